```python
import jax, jax.numpy as jnp
from jax import lax
import numpy as np

D_MODEL = 1024
BATCH = 4
SEQ = 8192
DEPTH = 2

CHUNK = 64
N_META = 16
EPS = 1e-6

POOL_GROUPS = 4
POOL_WINDOWS = (2, 4, 8, 16)
POOL_WIDTH = D_MODEL // 4
POOL_GDIM = POOL_WIDTH // POOL_GROUPS
CONV_WIDTH = D_MODEL // 4
CONV_KSIZE = 31
FOX_HEADS = 8
FOX_HEAD_DIM = 64
FOX_WIDTH = FOX_HEADS * FOX_HEAD_DIM
Q_BLOCK = 128
N_BRANCH = 3
IN_SPLITS = (POOL_WIDTH, CONV_WIDTH, CONV_WIDTH, FOX_WIDTH, FOX_WIDTH, FOX_WIDTH, FOX_HEADS, N_BRANCH * D_MODEL)
N_IN = sum(IN_SPLITS)
N_GROUPS = 4
EXPERTS_PER_GROUP = 4
N_EXPERTS = N_GROUPS * EXPERTS_PER_GROUP
TOP_K = 2
EXPERT_HIDDEN = D_MODEL // 4

kernel_name = "hybrid_pool_conv_fox_hmoe_meta"


def rms_norm(x, g):
    xf = x.astype(jnp.float32)
    y = xf * lax.rsqrt(jnp.mean(xf * xf, axis=-1, keepdims=True) + EPS)
    return (y * g.astype(jnp.float32)).astype(x.dtype)


def layer_norm(x, g, b):
    xf = x.astype(jnp.float32)
    mu = jnp.mean(xf, axis=-1, keepdims=True)
    var = jnp.mean(jnp.square(xf - mu), axis=-1, keepdims=True)
    y = (xf - mu) * lax.rsqrt(var + EPS)
    return (y * g.astype(jnp.float32) + b.astype(jnp.float32)).astype(x.dtype)


def pool_mixer(a, pool_w, pool_b, pool_scale):
    B_, L, _ = a.shape
    af = a.astype(jnp.float32).reshape(B_, L, POOL_GROUPS, POOL_GDIM)
    cs = jnp.concatenate([jnp.zeros_like(af[:, :1]), jnp.cumsum(af, axis=1)], axis=1)
    t = jnp.arange(L, dtype=jnp.int32)[:, None]
    win = jnp.array(POOL_WINDOWS, dtype=jnp.int32)[None, :]
    lo = jnp.maximum(t + 1 - win, 0)
    cnt = (t + 1 - lo).astype(jnp.float32)
    grp = jnp.arange(POOL_GROUPS, dtype=jnp.int32)[None, :]
    mean = (cs[:, 1:] - cs[:, lo, grp]) / cnt[None, :, :, None]
    diff = mean - af
    y = jnp.einsum('blgc,gcd->blgd', diff, pool_w.astype(jnp.float32)) + pool_b.astype(jnp.float32)
    y = y.reshape(B_, L, POOL_WIDTH) * pool_scale.astype(jnp.float32)
    return y.astype(a.dtype)


def conv_module(bv, bg, conv_w, conv_b, ln_g, ln_b):
    glu = bv * jax.nn.sigmoid(bg)
    y = lax.conv_general_dilated(
        glu, conv_w.astype(glu.dtype)[:, None, :], window_strides=(1,),
        padding=[(CONV_KSIZE - 1, 0)], dimension_numbers=('NWC', 'WIO', 'NWC'),
        feature_group_count=CONV_WIDTH)
    y = y + conv_b.astype(y.dtype)
    y = layer_norm(y, ln_g, ln_b)
    return jax.nn.silu(y)


def forgetting_attention(q, k, v, logf):
    B_, L, H, dh = q.shape
    n_blk = -(-L // Q_BLOCK)
    Lp = n_blk * Q_BLOCK
    pad = Lp - L
    qh = jnp.pad(q.transpose(0, 2, 1, 3), ((0, 0), (0, 0), (0, pad), (0, 0)))
    kh = jnp.pad(k.transpose(0, 2, 1, 3), ((0, 0), (0, 0), (0, pad), (0, 0)))
    vh = jnp.pad(v.transpose(0, 2, 1, 3), ((0, 0), (0, 0), (0, pad), (0, 0)))
    F = jnp.cumsum(jnp.pad(logf.transpose(0, 2, 1), ((0, 0), (0, 0), (0, pad))), axis=-1)
    qb = qh.reshape(B_, H, n_blk, Q_BLOCK, dh).transpose(2, 0, 1, 3, 4)
    Fq = F.reshape(B_, H, n_blk, Q_BLOCK).transpose(2, 0, 1, 3)
    qpos = jnp.arange(Lp, dtype=jnp.int32).reshape(n_blk, Q_BLOCK)
    kpos = jnp.arange(Lp, dtype=jnp.int32)
    scale = FOX_HEAD_DIM ** -0.5

    def block(args):
        qi, Fi, pi = args
        s = jnp.einsum('bhqd,bhkd->bhqk', qi, kh, preferred_element_type=jnp.float32) * scale
        s = s + Fi[..., None] - F[:, :, None, :]
        s = jnp.where(kpos[None, :] <= pi[:, None], s, -jnp.inf)
        p = jax.nn.softmax(s, axis=-1)
        return jnp.einsum('bhqk,bhkd->bhqd', p.astype(vh.dtype), vh)

    o = lax.map(block, (qb, Fq, qpos))
    o = o.transpose(1, 0, 3, 2, 4).reshape(B_, Lp, H * dh)
    return o[:, :L]


def mixer_layer(u, w_in, b_forget, pool_w, pool_b, pool_scale, conv_w, conv_b, conv_ln_g, conv_ln_b,
                w_out_a, w_out_b, w_out_c, w_o):
    B_, L, D = u.shape
    p = u @ w_in
    a, bv, bg, q, k, v, fg, gates = jnp.split(p, [int(i) for i in np.cumsum(IN_SPLITS)[:-1]], axis=-1)
    ya = pool_mixer(a, pool_w, pool_b, pool_scale) @ w_out_a
    yb = conv_module(bv, bg, conv_w, conv_b, conv_ln_g, conv_ln_b) @ w_out_b
    logf = jax.nn.log_sigmoid((fg + b_forget).astype(jnp.float32))
    heads = lambda z: z.reshape(B_, L, FOX_HEADS, FOX_HEAD_DIM)
    yc = forgetting_attention(heads(q), heads(k), heads(v), logf) @ w_out_c
    g = jax.nn.sigmoid(gates.reshape(B_, L, N_BRANCH, D))
    merged = g[:, :, 0] * ya + g[:, :, 1] * yb + g[:, :, 2] * yc
    return merged @ w_o


def hier_moe(h, router_g, router_g_b, router_e, router_e_b, exp_w1, exp_w3, exp_w2):
    B_, L, D = h.shape
    t = h.reshape(B_ * L, D)
    T = t.shape[0]
    glog = (t @ router_g).astype(jnp.float32) + router_g_b.astype(jnp.float32)
    gprob = jax.nn.softmax(glog, axis=-1)
    _, gsel = lax.top_k(glog, 1)
    gw = jnp.take_along_axis(gprob, gsel, axis=-1)
    elog = ((t @ router_e).astype(jnp.float32) + router_e_b.astype(jnp.float32)).reshape(T, N_GROUPS, EXPERTS_PER_GROUP)
    elog_sel = jnp.take_along_axis(elog, gsel[:, :, None], axis=1)[:, 0]
    eprob = jax.nn.softmax(elog_sel, axis=-1)
    top_p, top_i = lax.top_k(eprob, TOP_K)
    top_p = top_p / jnp.sum(top_p, axis=-1, keepdims=True)
    w_grp = jnp.sum(jax.nn.one_hot(top_i, EXPERTS_PER_GROUP, dtype=jnp.float32) * top_p[..., None], axis=1)
    comb = (jax.nn.one_hot(gsel[:, 0], N_GROUPS, dtype=jnp.float32)[:, :, None] * w_grp[:, None, :] * gw[:, :, None])
    comb = comb.reshape(T, N_EXPERTS).astype(t.dtype)
    y = jnp.zeros_like(t)
    for e in range(N_EXPERTS):
        he = jax.nn.silu(t @ exp_w1[e]) * (t @ exp_w3[e])
        y = y + comb[:, e:e + 1] * (he @ exp_w2[e])
    return y.reshape(B_, L, D)


def setup_inputs(seed: int = 0) -> dict:
    key = jax.random.key(seed)
    ks = jax.random.split(key, 32)
    f32 = jnp.float32
    nrm = lambda k, shape, s: (jax.random.normal(k, shape, f32) * s).astype(f32)
    D = D_MODEL
    return {
        "x": nrm(ks[0], (BATCH, SEQ, D), 1.0),
        "meta": nrm(ks[1], (N_META, D), 1.0),
        "norm1_g": 1.0 + nrm(ks[2], (DEPTH, D), 0.02),
        "w_in": nrm(ks[3], (DEPTH, D, N_IN), D ** -0.5),
        "b_forget": jax.random.uniform(ks[4], (DEPTH, FOX_HEADS), f32, 1.0, 6.0),
        "pool_w": nrm(ks[5], (DEPTH, POOL_GROUPS, POOL_GDIM, POOL_GDIM), POOL_GDIM ** -0.5),
        "pool_b": nrm(ks[6], (DEPTH, POOL_GROUPS, POOL_GDIM), 0.01),
        "pool_scale": 1.0 + nrm(ks[7], (DEPTH, POOL_WIDTH), 0.1),
        "conv_w": nrm(ks[8], (DEPTH, CONV_KSIZE, CONV_WIDTH), CONV_KSIZE ** -0.5),
        "conv_b": nrm(ks[9], (DEPTH, CONV_WIDTH), 0.01),
        "conv_ln_g": 1.0 + nrm(ks[10], (DEPTH, CONV_WIDTH), 0.02),
        "conv_ln_b": nrm(ks[11], (DEPTH, CONV_WIDTH), 0.01),
        "w_out_a": nrm(ks[12], (DEPTH, POOL_WIDTH, D), POOL_WIDTH ** -0.5),
        "w_out_b": nrm(ks[13], (DEPTH, CONV_WIDTH, D), CONV_WIDTH ** -0.5),
        "w_out_c": nrm(ks[14], (DEPTH, FOX_WIDTH, D), FOX_WIDTH ** -0.5),
        "w_o": nrm(ks[15], (DEPTH, D, D), D ** -0.5),
        "norm2_g": 1.0 + nrm(ks[16], (DEPTH, D), 0.02),
        "router_g": nrm(ks[17], (DEPTH, D, N_GROUPS), D ** -0.5),
        "router_g_b": nrm(ks[18], (DEPTH, N_GROUPS), 0.01),
        "router_e": nrm(ks[19], (DEPTH, D, N_EXPERTS), D ** -0.5),
        "router_e_b": nrm(ks[20], (DEPTH, N_EXPERTS), 0.01),
        "exp_w1": nrm(ks[21], (DEPTH, N_EXPERTS, D, EXPERT_HIDDEN), D ** -0.5),
        "exp_w3": nrm(ks[22], (DEPTH, N_EXPERTS, D, EXPERT_HIDDEN), D ** -0.5),
        "exp_w2": nrm(ks[23], (DEPTH, N_EXPERTS, EXPERT_HIDDEN, D), EXPERT_HIDDEN ** -0.5),
        "final_g": 1.0 + nrm(ks[24], (D,), 0.02),
    }


def reference(x, meta, norm1_g, w_in, b_forget, pool_w, pool_b, pool_scale, conv_w, conv_b, conv_ln_g,
              conv_ln_b, w_out_a, w_out_b, w_out_c, w_o, norm2_g, router_g, router_g_b, router_e,
              router_e_b, exp_w1, exp_w3, exp_w2, final_g):
    B_ = x.shape[0]
    m = jnp.broadcast_to(meta.astype(x.dtype)[None], (B_, N_META, D_MODEL))
    h = jnp.concatenate([m, x], axis=1)
    for l in range(DEPTH):
        u = rms_norm(h, norm1_g[l])
        h = h + mixer_layer(u, w_in[l], b_forget[l], pool_w[l], pool_b[l], pool_scale[l], conv_w[l],
                            conv_b[l], conv_ln_g[l], conv_ln_b[l], w_out_a[l], w_out_b[l], w_out_c[l], w_o[l])
        v = rms_norm(h, norm2_g[l])
        h = h + hier_moe(v, router_g[l], router_g_b[l], router_e[l], router_e_b[l], exp_w1[l], exp_w3[l], exp_w2[l])
    return rms_norm(h[:, N_META:], final_g)
```

```python
import functools

import jax
import jax.numpy as jnp
from jax import lax
from jax.experimental import pallas as pl
from jax.experimental.pallas import tpu as pltpu

D = 1024
N_META = 16
EPS = 1e-6
POOL_W = 256
POOL_WINDOWS = (2, 4, 8, 16)
CONV_W = 256
CONV_K = 31
HEADS = 8
HEAD_DIM = 64
FOX_W = HEADS * HEAD_DIM
N_GROUPS = 4
EPG = 4
N_EXPERTS = 16
EXPERT_HIDDEN = 256
GROUP_HIDDEN = EPG * EXPERT_HIDDEN

LANES = 128
LP = 8320
TM = 640
BLOCKS_PER_SEQ = LP // TM
HALO = 32
TQ = 640
VMEM_LIMIT = 56 * 1024 * 1024

F32 = jnp.float32
BF16 = jnp.bfloat16


def _dot(a, b):
    return jnp.dot(a, b, preferred_element_type=F32)


def _rms_norm(x, g):
    ms = jnp.mean(x * x, axis=-1, keepdims=True)
    return x * lax.rsqrt(ms + EPS) * g


def _split_bf16(x):
    hi = x.astype(BF16)
    lo = (x - hi.astype(F32)).astype(BF16)
    return hi, lo


def _in_proj_kernel(h_ref, g_ref, wabc_ref, wqkv_ref, wf2_ref, bf_ref, tri_ref,
                    a_ref, glu_ref, q_ref, k_ref, v_ref, f_ref, carry_ref):
    i = pl.program_id(0)

    @pl.when(i % BLOCKS_PER_SEQ == 0)
    def _():
        carry_ref[...] = jnp.zeros_like(carry_ref)

    un = _rms_norm(h_ref[...], g_ref[...])
    u, u_lo = _split_bf16(un)

    abc = _dot(u, wabc_ref[...])
    a_ref[...] = abc[:, :POOL_W].astype(BF16)
    glu = abc[:, POOL_W:POOL_W + CONV_W] * jax.nn.sigmoid(abc[:, POOL_W + CONV_W:])
    glu_ref[...] = glu.astype(BF16)

    qkv = _dot(u, wqkv_ref[...])
    q_ref[...] = (qkv[:, :FOX_W] * (HEAD_DIM ** -0.5)).astype(BF16)
    k_ref[...] = qkv[:, FOX_W:2 * FOX_W].astype(BF16)
    v_ref[...] = qkv[:, 2 * FOX_W:].astype(BF16)

    wf2 = wf2_ref[...]
    f2 = _dot(u, wf2)
    f1 = _dot(u_lo, wf2[:, :LANES])
    fg = f2[:, :LANES] + f2[:, LANES:] + f1 + bf_ref[...]
    logf = jnp.minimum(fg, 0.0) - jnp.log1p(jnp.exp(-jnp.abs(fg)))
    l_hi = logf.astype(BF16)
    r1 = logf - l_hi.astype(F32)
    l_mid = r1.astype(BF16)
    l_lo = (r1 - l_mid.astype(F32)).astype(BF16)
    tri = tri_ref[...]
    cum = _dot(tri, l_hi) + _dot(tri, l_mid) + _dot(tri, l_lo) + carry_ref[...]
    carry_ref[...] = cum[TM - 1:TM, :]
    f_ref[...] = cum[:, :HEADS]


def _in_proj(h, g, wabc, wqkv, wf2, bf, tri):
    tp = h.shape[0]
    nblk = tp // TM
    row = lambda w: pl.BlockSpec((TM, w), lambda i: (i, 0))
    const = lambda s: pl.BlockSpec(s, lambda i: (0, 0))
    return pl.pallas_call(
        _in_proj_kernel,
        grid=(nblk,),
        in_specs=[row(D), const((1, D)), const(wabc.shape), const(wqkv.shape), const(wf2.shape),
                  const((1, LANES)), const((TM, TM))],
        out_specs=[row(POOL_W), row(CONV_W), row(FOX_W), row(FOX_W), row(FOX_W), row(HEADS)],
        out_shape=[jax.ShapeDtypeStruct((tp, POOL_W), BF16), jax.ShapeDtypeStruct((tp, CONV_W), BF16),
                   jax.ShapeDtypeStruct((tp, FOX_W), BF16), jax.ShapeDtypeStruct((tp, FOX_W), BF16),
                   jax.ShapeDtypeStruct((tp, FOX_W), BF16), jax.ShapeDtypeStruct((tp, HEADS), F32)],
        scratch_shapes=[pltpu.VMEM((1, LANES), F32)],
        compiler_params=pltpu.CompilerParams(dimension_semantics=("arbitrary",),
                                             vmem_limit_bytes=VMEM_LIMIT),
        name="in_proj",
    )(h, g, wabc, wqkv, wf2, bf, tri)


def _attn_kernel(q_ref, k_ref, v_ref, fq_ref, fk_ref, o_ref, m_ref, l_ref, acc_ref):
    qi = pl.program_id(1)
    q = q_ref[0]
    fq = fq_ref[0]
    m_ref[...] = jnp.full_like(m_ref, -jnp.inf)
    l_ref[...] = jnp.zeros_like(l_ref)
    acc_ref[...] = jnp.zeros_like(acc_ref)

    def tile(j, masked):
        start = pl.multiple_of(j * TQ, TQ)
        k = k_ref[0, pl.ds(start, TQ), :]
        v = v_ref[0, pl.ds(start, TQ), :]
        fk = fk_ref[0, :, pl.ds(start, TQ)]
        s = lax.dot_general(q, k, (((1,), (1,)), ((), ())), preferred_element_type=F32)
        s = s + fq - fk
        if masked:
            row = lax.broadcasted_iota(jnp.int32, (TQ, TQ), 0)
            col = lax.broadcasted_iota(jnp.int32, (TQ, TQ), 1)
            s = jnp.where(col <= row, s, -jnp.inf)
        m_old = m_ref[...]
        m_new = jnp.maximum(m_old, jnp.max(s, axis=-1, keepdims=True))
        alpha = jnp.exp(m_old - m_new)
        p = jnp.exp(s - m_new)
        l_ref[...] = alpha * l_ref[...] + jnp.sum(p, axis=-1, keepdims=True)
        acc_ref[...] = alpha * acc_ref[...] + _dot(p.astype(BF16), v)
        m_ref[...] = m_new

    def body(j, c):
        tile(j, False)
        return c

    lax.fori_loop(0, qi, body, 0)
    tile(qi, True)
    o_ref[0] = (acc_ref[...] / l_ref[...]).astype(BF16)


def _attention(q, k, v, fq, fk):
    bh = q.shape[0]
    nq = LP // TQ
    return pl.pallas_call(
        _attn_kernel,
        grid=(bh, nq),
        in_specs=[pl.BlockSpec((1, TQ, HEAD_DIM), lambda b, i: (b, i, 0)),
                  pl.BlockSpec((1, LP, HEAD_DIM), lambda b, i: (b, 0, 0)),
                  pl.BlockSpec((1, LP, HEAD_DIM), lambda b, i: (b, 0, 0)),
                  pl.BlockSpec((1, TQ, 1), lambda b, i: (b, i, 0)),
                  pl.BlockSpec((1, 1, LP), lambda b, i: (b, 0, 0))],
        out_specs=pl.BlockSpec((1, TQ, HEAD_DIM), lambda b, i: (b, i, 0)),
        out_shape=jax.ShapeDtypeStruct((bh, LP, HEAD_DIM), BF16),
        scratch_shapes=[pltpu.VMEM((TQ, 1), F32), pltpu.VMEM((TQ, 1), F32),
                        pltpu.VMEM((TQ, HEAD_DIM), F32)],
        compiler_params=pltpu.CompilerParams(dimension_semantics=("arbitrary", "arbitrary"),
                                             vmem_limit_bytes=VMEM_LIMIT),
        name="fox_attention",
    )(q, k, v, fq, fk)


def _mixer_out_kernel(h_ref, g_ref, a_ref, ah_ref, glu_ref, gh_ref, o_ref, wg_ref,
                      pw_ref, pb_ref, ps_ref, cw_ref, cb_ref, lg_ref, lb_ref,
                      woa_ref, wob_ref, woc_ref, wo_ref, out_ref, xa_ref, xg_ref):
    i = pl.program_id(0)
    blk = i % BLOCKS_PER_SEQ
    keep = (blk > 0).astype(F32)

    h = h_ref[...]
    u = _rms_norm(h, g_ref[...]).astype(BF16)

    xa_ref[0:HALO, :] = ah_ref[...].astype(F32) * keep
    xa_ref[HALO:, :] = a_ref[...].astype(F32)
    pos = blk * TM + lax.broadcasted_iota(jnp.int32, (TM, LANES), 0) + 1
    lane = lax.broadcasted_iota(jnp.int32, (TM, LANES), 1)
    low = lane < (LANES // 2)
    means = []
    for t in range(2):
        x0 = xa_ref[HALO:, t * LANES:(t + 1) * LANES]
        w_small, w_big = POOL_WINDOWS[2 * t], POOL_WINDOWS[2 * t + 1]
        s = x0
        for d in range(1, w_small):
            s = s + xa_ref[HALO - d:HALO - d + TM, t * LANES:(t + 1) * LANES]
        s_small = s
        for d in range(w_small, w_big):
            s = s + xa_ref[HALO - d:HALO - d + TM, t * LANES:(t + 1) * LANES]
        cnt = jnp.minimum(pos, jnp.where(low, w_small, w_big)).astype(F32)
        means.append(jnp.where(low, s_small, s) / cnt - x0)
    diff = jnp.concatenate(means, axis=1).astype(BF16)
    ya = (_dot(diff, pw_ref[...]) + pb_ref[...]) * ps_ref[...]
    ya = _dot(ya.astype(BF16), woa_ref[...])

    xg_ref[0:HALO, :] = gh_ref[...].astype(F32) * keep
    xg_ref[HALO:, :] = glu_ref[...].astype(F32)
    cw = cw_ref[...]
    acc = jnp.zeros((TM, CONV_W), F32) + cb_ref[...]
    base = HALO - (CONV_K - 1)
    for j in range(CONV_K):
        acc = acc + xg_ref[base + j:base + j + TM, :] * cw[j:j + 1, :]
    mu = jnp.mean(acc, axis=-1, keepdims=True)
    cen = acc - mu
    var = jnp.mean(cen * cen, axis=-1, keepdims=True)
    yb = cen * lax.rsqrt(var + EPS) * lg_ref[...] + lb_ref[...]
    yb = yb * jax.nn.sigmoid(yb)
    yb = _dot(yb.astype(BF16), wob_ref[...])

    yc = _dot(o_ref[...], woc_ref[...])
    merged = jax.nn.sigmoid(_dot(u, wg_ref[:, 0:D])) * ya
    merged = merged + jax.nn.sigmoid(_dot(u, wg_ref[:, D:2 * D])) * yb
    merged = merged + jax.nn.sigmoid(_dot(u, wg_ref[:, 2 * D:3 * D])) * yc
    out_ref[...] = h + _dot(merged.astype(BF16), wo_ref[...])


def _mixer_out(h, g, a, glu, o, wg, pw, pb, ps, cw, cb, lg, lb, woa, wob, woc, wo):
    tp = h.shape[0]
    nblk = tp // TM
    row = lambda w: pl.BlockSpec((TM, w), lambda i: (i, 0))
    halo = lambda w: pl.BlockSpec((HALO, w), lambda i: (jnp.maximum(i * (TM // HALO) - 1, 0), 0))
    const = lambda x: pl.BlockSpec(x.shape, lambda i: (0, 0))
    return pl.pallas_call(
        _mixer_out_kernel,
        grid=(nblk,),
        in_specs=[row(D), const(g), row(POOL_W), halo(POOL_W), row(CONV_W), halo(CONV_W), row(FOX_W),
                  const(wg), const(pw), const(pb), const(ps), const(cw), const(cb), const(lg), const(lb),
                  const(woa), const(wob), const(woc), const(wo)],
        out_specs=row(D),
        out_shape=jax.ShapeDtypeStruct((tp, D), F32),
        scratch_shapes=[pltpu.VMEM((TM + HALO, POOL_W), F32), pltpu.VMEM((TM + HALO, CONV_W), F32)],
        compiler_params=pltpu.CompilerParams(dimension_semantics=("arbitrary",),
                                             vmem_limit_bytes=VMEM_LIMIT),
        name="mixer_out",
    )(h, g, a, a, glu, glu, o, wg, pw, pb, ps, cw, cb, lg, lb, woa, wob, woc, wo)


def _moe_kernel(h_ref, g_ref, wr2_ref, rb_ref, w1_ref, w3_ref, w2_ref, out_ref, v_ref, comb_ref):
    c = pl.program_id(1)

    @pl.when(c == 0)
    def _():
        vn = _rms_norm(h_ref[...], g_ref[...])
        v_hi, v_lo = _split_bf16(vn)
        v_ref[...] = v_hi
        wr2 = wr2_ref[...]
        r2 = _dot(v_hi, wr2)
        logits = r2[:, :LANES] + r2[:, LANES:] + _dot(v_lo, wr2[:, :LANES]) + rb_ref[...]
        lt = logits.T
        gl = [lt[g:g + 1, :] for g in range(N_GROUPS)]
        gmax = functools.reduce(jnp.maximum, gl)
        gsum = functools.reduce(lambda x, y: x + y, [jnp.exp(x - gmax) for x in gl])
        gw = 1.0 / gsum
        sel, taken = [], jnp.zeros_like(gmax, dtype=jnp.bool_)
        for g in range(N_GROUPS):
            s = jnp.logical_and(gl[g] == gmax, jnp.logical_not(taken))
            sel.append(s)
            taken = jnp.logical_or(taken, s)
        el = []
        for j in range(EPG):
            e = jnp.zeros_like(gmax)
            for g in range(N_GROUPS):
                e = jnp.where(sel[g], lt[N_GROUPS + g * EPG + j:N_GROUPS + g * EPG + j + 1, :], e)
            el.append(e)
        neg = jnp.full_like(gmax, -jnp.inf)
        top1 = functools.reduce(jnp.maximum, el)
        is1, taken = [], jnp.zeros_like(gmax, dtype=jnp.bool_)
        for j in range(EPG):
            s = jnp.logical_and(el[j] == top1, jnp.logical_not(taken))
            is1.append(s)
            taken = jnp.logical_or(taken, s)
        rest = [jnp.where(is1[j], neg, el[j]) for j in range(EPG)]
        top2 = functools.reduce(jnp.maximum, rest)
        is2, taken = [], jnp.zeros_like(gmax, dtype=jnp.bool_)
        for j in range(EPG):
            s = jnp.logical_and(jnp.logical_and(rest[j] == top2, jnp.logical_not(is1[j])),
                                jnp.logical_not(taken))
            is2.append(s)
            taken = jnp.logical_or(taken, s)
        w1 = 1.0 / (1.0 + jnp.exp(top2 - top1))
        w2 = 1.0 - w1
        rows = []
        for g in range(N_GROUPS):
            for j in range(EPG):
                wj = jnp.where(is1[j], w1, jnp.where(is2[j], w2, 0.0))
                rows.append(jnp.where(sel[g], wj * gw, 0.0))
        rows.append(jnp.zeros((LANES - N_EXPERTS, TM), F32))
        comb_ref[...] = jnp.concatenate(rows, axis=0).T
        out_ref[...] = h_ref[...]

    v = v_ref[...]
    h1 = _dot(v, w1_ref[0])
    h3 = _dot(v, w3_ref[0])
    he = h1 * jax.nn.sigmoid(h1) * h3
    comb = comb_ref[...]
    lane = lax.broadcasted_iota(jnp.int32, (TM, LANES), 1)
    parts = []
    for j in range(EPG):
        wcol = jnp.sum(jnp.where(lane == c * EPG + j, comb, 0.0), axis=-1, keepdims=True)
        parts.append((he[:, j * EXPERT_HIDDEN:(j + 1) * EXPERT_HIDDEN] * wcol).astype(BF16))
    out_ref[...] += _dot(jnp.concatenate(parts, axis=1), w2_ref[0])


def _moe(h, g, wr2, rb, w1g, w3g, w2g):
    tp = h.shape[0]
    nblk = tp // TM
    const = lambda x: pl.BlockSpec(x.shape, lambda i, c: (0, 0))
    wspec = pl.BlockSpec((1, D, GROUP_HIDDEN), lambda i, c: (c, 0, 0))
    return pl.pallas_call(
        _moe_kernel,
        grid=(nblk, N_GROUPS),
        in_specs=[pl.BlockSpec((TM, D), lambda i, c: (i, 0)), const(g), const(wr2), const(rb),
                  wspec, wspec, pl.BlockSpec((1, GROUP_HIDDEN, D), lambda i, c: (c, 0, 0))],
        out_specs=pl.BlockSpec((TM, D), lambda i, c: (i, 0)),
        out_shape=jax.ShapeDtypeStruct((tp, D), F32),
        scratch_shapes=[pltpu.VMEM((TM, D), BF16), pltpu.VMEM((TM, LANES), F32)],
        compiler_params=pltpu.CompilerParams(dimension_semantics=("arbitrary", "arbitrary"),
                                             vmem_limit_bytes=VMEM_LIMIT),
        name="hier_moe",
    )(h, g, wr2, rb, w1g, w3g, w2g)


def _final_norm_kernel(h_ref, g_ref, out_ref):
    out_ref[...] = _rms_norm(h_ref[...], g_ref[...])


def _final_norm(h, g):
    tp = h.shape[0]
    return pl.pallas_call(
        _final_norm_kernel,
        grid=(tp // TM,),
        in_specs=[pl.BlockSpec((TM, D), lambda i: (i, 0)), pl.BlockSpec((1, D), lambda i: (0, 0))],
        out_specs=pl.BlockSpec((TM, D), lambda i: (i, 0)),
        out_shape=jax.ShapeDtypeStruct((tp, D), F32),
        compiler_params=pltpu.CompilerParams(dimension_semantics=("arbitrary",)),
        name="final_norm",
    )(h, g)


def _pad_lanes(x, n=LANES):
    return jnp.pad(x, ((0, 0), (0, n - x.shape[1])))


def _hi_lo_weight(w):
    hi = w.astype(BF16)
    lo = (w - hi.astype(F32)).astype(BF16)
    return jnp.concatenate([_pad_lanes(hi), _pad_lanes(lo)], axis=1)


def kernel(x, meta, norm1_g, w_in, b_forget, pool_w, pool_b, pool_scale, conv_w, conv_b, conv_ln_g,
           conv_ln_b, w_out_a, w_out_b, w_out_c, w_o, norm2_g, router_g, router_g_b, router_e,
           router_e_b, exp_w1, exp_w3, exp_w2, final_g):
    B, L, _ = x.shape
    depth = w_in.shape[0]
    seq = N_META + L
    assert seq <= LP and LP % TM == 0 and LP % TQ == 0
    m = jnp.broadcast_to(meta.astype(x.dtype)[None], (B, N_META, D))
    h = jnp.concatenate([m, x, jnp.zeros((B, LP - seq, D), x.dtype)], axis=1).reshape(B * LP, D)
    tri = jnp.tril(jnp.ones((TM, TM), BF16))

    c_abc = POOL_W + 2 * CONV_W
    c_qkv = c_abc + 3 * FOX_W
    c_f = c_qkv + HEADS
    for l in range(depth):
        wl = w_in[l]
        wabc = wl[:, :c_abc].astype(BF16)
        wqkv = wl[:, c_abc:c_qkv].astype(BF16)
        wf2 = _hi_lo_weight(wl[:, c_qkv:c_f])
        wg = wl[:, c_f:].astype(BF16)
        bf = _pad_lanes(b_forget[l][None, :])
        a, glu, q, k, v, fcum = _in_proj(h, norm1_g[l][None, :], wabc, wqkv, wf2, bf, tri)

        heads = lambda z: z.reshape(B, LP, HEADS, HEAD_DIM).transpose(0, 2, 1, 3).reshape(B * HEADS, LP, HEAD_DIM)
        fh = fcum.reshape(B, LP, HEADS).transpose(0, 2, 1).reshape(B * HEADS, LP)
        o = _attention(heads(q), heads(k), heads(v), fh[:, :, None], fh[:, None, :])
        o = o.reshape(B, HEADS, LP, HEAD_DIM).transpose(0, 2, 1, 3).reshape(B * LP, FOX_W)

        pw = jax.scipy.linalg.block_diag(*[pool_w[l, g] for g in range(pool_w.shape[1])]).astype(BF16)
        h = _mixer_out(h, norm1_g[l][None, :], a, glu, o, wg, pw, pool_b[l].reshape(1, POOL_W),
                       pool_scale[l][None, :], _pad_rows(conv_w[l]), conv_b[l][None, :],
                       conv_ln_g[l][None, :], conv_ln_b[l][None, :],
                       w_out_a[l].astype(BF16), w_out_b[l].astype(BF16), w_out_c[l].astype(BF16),
                       w_o[l].astype(BF16))

        wr2 = _hi_lo_weight(jnp.concatenate([router_g[l], router_e[l]], axis=1))
        rb = _pad_lanes(jnp.concatenate([router_g_b[l], router_e_b[l]])[None, :])
        w1g = exp_w1[l].reshape(N_GROUPS, EPG, D, EXPERT_HIDDEN).transpose(0, 2, 1, 3).reshape(N_GROUPS, D, GROUP_HIDDEN)
        w3g = exp_w3[l].reshape(N_GROUPS, EPG, D, EXPERT_HIDDEN).transpose(0, 2, 1, 3).reshape(N_GROUPS, D, GROUP_HIDDEN)
        w2g = exp_w2[l].reshape(N_GROUPS, GROUP_HIDDEN, D)
        h = _moe(h, norm2_g[l][None, :], wr2, rb, w1g.astype(BF16), w3g.astype(BF16), w2g.astype(BF16))

    out = _final_norm(h, final_g[None, :])
    return out.reshape(B, LP, D)[:, N_META:seq]


def _pad_rows(w, n=32):
    return jnp.pad(w, ((0, n - w.shape[0]), (0, 0)))
```

```python
import functools
import math

import jax
import jax.numpy as jnp
from jax import lax
from jax.experimental import pallas as pl
from jax.experimental.pallas import tpu as pltpu

D = 1024
N_META = 16
EPS = 1e-6
POOL_W = 256
POOL_WINDOWS = (2, 4, 8, 16)
CONV_W = 256
CONV_K = 31
HEADS = 8
HEAD_DIM = 64
FOX_W = HEADS * HEAD_DIM
N_GROUPS = 4
EPG = 4
N_EXPERTS = 16
EXPERT_HIDDEN = 256
GROUP_HIDDEN = EPG * EXPERT_HIDDEN

LANES = 128
HEAD_PAD = LANES
ATT_W = HEADS * HEAD_PAD
LP = 8704
TM = 512
BLOCKS_PER_SEQ = LP // TM
HALO = 32
TQ = 512
N_FPIECE = 3
VMEM_LIMIT = 56 * 1024 * 1024
LOG2E = math.log2(math.e)

F32 = jnp.float32
BF16 = jnp.bfloat16


def _dot(a, b):
    return jnp.dot(a, b, preferred_element_type=F32)


def _dot_nt(a, b):
    return lax.dot_general(a, b, (((1,), (1,)), ((), ())), preferred_element_type=F32)


def _rms_norm(x, g):
    ms = jnp.mean(x * x, axis=-1, keepdims=True)
    return x * lax.rsqrt(ms + EPS) * g


def _split_bf16(x):
    hi = x.astype(BF16)
    lo = (x - hi.astype(F32)).astype(BF16)
    return hi, lo


def _in_proj_kernel(h_ref, g_ref, wabc_ref, wqk_ref, wvt_ref, wf2_ref, bf_ref, tri_ref, sel_ref,
                    a_ref, glu_ref, q_ref, k_ref, vt_ref, carry_ref):
    i = pl.program_id(0)

    @pl.when(i % BLOCKS_PER_SEQ == 0)
    def _():
        carry_ref[...] = jnp.zeros_like(carry_ref)

    un = _rms_norm(h_ref[...], g_ref[...])
    u, u_lo = _split_bf16(un)

    abc = _dot(u, wabc_ref[...])
    a_ref[...] = abc[:, :POOL_W].astype(BF16)
    glu = abc[:, POOL_W:POOL_W + CONV_W] * jax.nn.sigmoid(abc[:, POOL_W + CONV_W:])
    glu_ref[...] = glu.astype(BF16)

    wf2 = wf2_ref[...]
    f2 = _dot(u, wf2)
    f1 = _dot(u_lo, wf2[:, :LANES])
    fg = f2[:, :LANES] + f2[:, LANES:] + f1 + bf_ref[...]
    logf = jnp.minimum(fg, 0.0) - jnp.log1p(jnp.exp(-jnp.abs(fg)))
    l_hi = logf.astype(BF16)
    r1 = logf - l_hi.astype(F32)
    l_mid = r1.astype(BF16)
    l_lo = (r1 - l_mid.astype(F32)).astype(BF16)
    tri = tri_ref[...]
    cum = _dot(tri, l_hi) + _dot(tri, l_mid) + _dot(tri, l_lo) + carry_ref[...]
    carry_ref[...] = cum[TM - 1:TM, :]

    lane = lax.broadcasted_iota(jnp.int32, (TM, LANES), 1)
    fv = jnp.where(lane < HEADS, cum * LOG2E, 0.0)
    f_hi = fv.astype(BF16).astype(F32)
    r1 = fv - f_hi
    f_mid = r1.astype(BF16).astype(F32)
    f_lo = (r1 - f_mid).astype(BF16).astype(F32)
    fcat = f_hi + pltpu.roll(f_mid, HEADS, 1) + pltpu.roll(f_lo, 2 * HEADS, 1)
    fcat = jnp.where(lane == N_FPIECE * HEADS, 1.0, fcat).astype(BF16)

    qk = _dot(u, wqk_ref[...]) + _dot(fcat, sel_ref[...])
    q_ref[...] = qk[:, :ATT_W].astype(BF16)
    k_ref[...] = qk[:, ATT_W:].astype(BF16)

    vt = _dot_nt(wvt_ref[...], u)
    row = lax.broadcasted_iota(jnp.int32, (ATT_W, TM), 0)
    vt_ref[0] = jnp.where(row % HEAD_PAD == HEAD_DIM, 1.0, vt).astype(BF16)


def _in_proj(h, g, wabc, wqk, wvt, wf2, bf, tri, sel, batch):
    tp = h.shape[0]
    nblk = tp // TM
    row = lambda w: pl.BlockSpec((TM, w), lambda i: (i, 0))
    const = lambda x: pl.BlockSpec(x.shape, lambda i: (0, 0))
    return pl.pallas_call(
        _in_proj_kernel,
        grid=(nblk,),
        in_specs=[row(D), const(g), const(wabc), const(wqk), const(wvt), const(wf2), const(bf),
                  const(tri), const(sel)],
        out_specs=[row(POOL_W), row(CONV_W), row(ATT_W), row(ATT_W),
                   pl.BlockSpec((1, ATT_W, TM), lambda i: (i // BLOCKS_PER_SEQ, 0, i % BLOCKS_PER_SEQ))],
        out_shape=[jax.ShapeDtypeStruct((tp, POOL_W), BF16), jax.ShapeDtypeStruct((tp, CONV_W), BF16),
                   jax.ShapeDtypeStruct((tp, ATT_W), BF16), jax.ShapeDtypeStruct((tp, ATT_W), BF16),
                   jax.ShapeDtypeStruct((batch, ATT_W, LP), BF16)],
        scratch_shapes=[pltpu.VMEM((1, LANES), F32)],
        compiler_params=pltpu.CompilerParams(dimension_semantics=("arbitrary",),
                                             vmem_limit_bytes=VMEM_LIMIT),
        name="in_proj",
    )(h, g, wabc, wqk, wvt, wf2, bf, tri, sel)


HEADS_PER_STEP = 2


def _attn_kernel(q_ref, k_ref, vt_ref, o_ref, m_ref, acc_ref, s_ref, mt_ref):
    qi = pl.program_id(2)
    m_ref[...] = jnp.full_like(m_ref, -jnp.inf)
    acc_ref[...] = jnp.zeros_like(acc_ref)

    def scores(j, slot, masked=False):
        start = pl.multiple_of(j * TQ, TQ)
        for hd in range(HEADS_PER_STEP):
            cols = slice(hd * HEAD_PAD, (hd + 1) * HEAD_PAD)
            s = _dot_nt(k_ref[pl.ds(start, TQ), cols], q_ref[:, cols])
            if masked:
                key = lax.broadcasted_iota(jnp.int32, (TQ, TQ), 0)
                qry = lax.broadcasted_iota(jnp.int32, (TQ, TQ), 1)
                s = jnp.where(key <= qry, s, -jnp.inf)
            s_ref[slot, hd] = s
            mt_ref[slot, hd] = jnp.max(s, axis=0, keepdims=True)

    def values(j, slot):
        start = pl.multiple_of(j * TQ, TQ)
        for hd in range(HEADS_PER_STEP):
            cols = slice(hd * HEAD_PAD, (hd + 1) * HEAD_PAD)
            m_old = m_ref[hd]
            m_new = jnp.maximum(m_old, mt_ref[slot, hd])
            p = jnp.exp2(s_ref[slot, hd] - m_new).astype(BF16)
            alpha = jnp.exp2(m_old - m_new)
            pv = _dot(vt_ref[0, cols, pl.ds(start, TQ)], p)
            acc_ref[hd] = alpha * acc_ref[hd] + pv
            m_ref[hd] = m_new

    @pl.when(qi == 0)
    def _():
        scores(0, 0, masked=True)
        values(0, 0)

    @pl.when(qi > 0)
    def _():
        scores(0, 0)

        def body(t, c):
            j = 2 * t
            scores(j + 1, 1)
            values(j, 0)
            scores(j + 2, 0)
            values(j + 1, 1)
            return c

        last = qi - 1
        lax.fori_loop(0, last // 2, body, 0)

        @pl.when(last % 2 == 0)
        def _():
            scores(qi, 1, masked=True)
            values(last, 0)
            values(qi, 1)

        @pl.when(last % 2 == 1)
        def _():
            scores(last, 1)
            values(last - 1, 0)
            scores(qi, 0, masked=True)
            values(last, 1)
            values(qi, 0)

    outs = []
    for hd in range(HEADS_PER_STEP):
        acc = acc_ref[hd]
        outs.append(acc[:HEAD_DIM, :] / acc[HEAD_DIM:HEAD_DIM + 1, :])
    o_ref[...] = jnp.concatenate(outs, axis=0).T.astype(BF16)


def _attention(q, k, vt, batch):
    nq = LP // TQ
    w = HEADS_PER_STEP * HEAD_PAD
    return pl.pallas_call(
        _attn_kernel,
        grid=(batch, HEADS // HEADS_PER_STEP, nq),
        in_specs=[pl.BlockSpec((TQ, w), lambda b, h, i: (b * nq + i, h)),
                  pl.BlockSpec((LP, w), lambda b, h, i: (b, h)),
                  pl.BlockSpec((1, w, LP), lambda b, h, i: (b, h, 0))],
        out_specs=pl.BlockSpec((TQ, HEADS_PER_STEP * HEAD_DIM), lambda b, h, i: (b * nq + i, h)),
        out_shape=jax.ShapeDtypeStruct((batch * LP, FOX_W), BF16),
        scratch_shapes=[pltpu.VMEM((HEADS_PER_STEP, 1, TQ), F32),
                        pltpu.VMEM((HEADS_PER_STEP, HEAD_PAD, TQ), F32),
                        pltpu.VMEM((2, HEADS_PER_STEP, TQ, TQ), F32),
                        pltpu.VMEM((2, HEADS_PER_STEP, 1, TQ), F32)],
        compiler_params=pltpu.CompilerParams(dimension_semantics=("arbitrary", "arbitrary", "arbitrary"),
                                             vmem_limit_bytes=VMEM_LIMIT),
        name="fox_attention",
    )(q, k, vt)


def _mixer_out_kernel(h_ref, g_ref, a_ref, ah_ref, glu_ref, gh_ref, o_ref, wg_ref,
                      pw_ref, pb_ref, ps_ref, cw_ref, cb_ref, lg_ref, lb_ref,
                      woa_ref, wob_ref, woc_ref, wo_ref, out_ref, xa_ref, xg_ref):
    i = pl.program_id(0)
    blk = i % BLOCKS_PER_SEQ
    keep = (blk > 0).astype(F32)

    h = h_ref[...]
    u = _rms_norm(h, g_ref[...]).astype(BF16)

    xa_ref[0:HALO, :] = ah_ref[...].astype(F32) * keep
    xa_ref[HALO:, :] = a_ref[...].astype(F32)
    pos = blk * TM + lax.broadcasted_iota(jnp.int32, (TM, LANES), 0) + 1
    lane = lax.broadcasted_iota(jnp.int32, (TM, LANES), 1)
    low = lane < (LANES // 2)
    means = []
    for t in range(2):
        x0 = xa_ref[HALO:, t * LANES:(t + 1) * LANES]
        w_small, w_big = POOL_WINDOWS[2 * t], POOL_WINDOWS[2 * t + 1]
        s = x0
        for d in range(1, w_small):
            s = s + xa_ref[HALO - d:HALO - d + TM, t * LANES:(t + 1) * LANES]
        s_small = s
        for d in range(w_small, w_big):
            s = s + xa_ref[HALO - d:HALO - d + TM, t * LANES:(t + 1) * LANES]
        cnt = jnp.minimum(pos, jnp.where(low, w_small, w_big)).astype(F32)
        means.append(jnp.where(low, s_small, s) / cnt - x0)
    diff = jnp.concatenate(means, axis=1).astype(BF16)
    ya = (_dot(diff, pw_ref[...]) + pb_ref[...]) * ps_ref[...]
    ya = _dot(ya.astype(BF16), woa_ref[...])

    xg_ref[0:HALO, :] = gh_ref[...].astype(F32) * keep
    xg_ref[HALO:, :] = glu_ref[...].astype(F32)
    cw = cw_ref[...]
    acc = jnp.zeros((TM, CONV_W), F32) + cb_ref[...]
    base = HALO - (CONV_K - 1)
    for j in range(CONV_K):
        acc = acc + xg_ref[base + j:base + j + TM, :] * cw[j:j + 1, :]
    mu = jnp.mean(acc, axis=-1, keepdims=True)
    cen = acc - mu
    var = jnp.mean(cen * cen, axis=-1, keepdims=True)
    yb = cen * lax.rsqrt(var + EPS) * lg_ref[...] + lb_ref[...]
    yb = yb * jax.nn.sigmoid(yb)
    yb = _dot(yb.astype(BF16), wob_ref[...])

    yc = _dot(o_ref[...], woc_ref[...])
    merged = jax.nn.sigmoid(_dot(u, wg_ref[:, 0:D])) * ya
    merged = merged + jax.nn.sigmoid(_dot(u, wg_ref[:, D:2 * D])) * yb
    merged = merged + jax.nn.sigmoid(_dot(u, wg_ref[:, 2 * D:3 * D])) * yc
    out_ref[...] = h + _dot(merged.astype(BF16), wo_ref[...])


def _mixer_out(h, g, a, glu, o, wg, pw, pb, ps, cw, cb, lg, lb, woa, wob, woc, wo):
    tp = h.shape[0]
    nblk = tp // TM
    row = lambda w: pl.BlockSpec((TM, w), lambda i: (i, 0))
    halo = lambda w: pl.BlockSpec((HALO, w), lambda i: (jnp.maximum(i * (TM // HALO) - 1, 0), 0))
    const = lambda x: pl.BlockSpec(x.shape, lambda i: (0, 0))
    return pl.pallas_call(
        _mixer_out_kernel,
        grid=(nblk,),
        in_specs=[row(D), const(g), row(POOL_W), halo(POOL_W), row(CONV_W), halo(CONV_W), row(FOX_W),
                  const(wg), const(pw), const(pb), const(ps), const(cw), const(cb), const(lg), const(lb),
                  const(woa), const(wob), const(woc), const(wo)],
        out_specs=row(D),
        out_shape=jax.ShapeDtypeStruct((tp, D), F32),
        scratch_shapes=[pltpu.VMEM((TM + HALO, POOL_W), F32), pltpu.VMEM((TM + HALO, CONV_W), F32)],
        compiler_params=pltpu.CompilerParams(dimension_semantics=("arbitrary",),
                                             vmem_limit_bytes=VMEM_LIMIT),
        name="mixer_out",
    )(h, g, a, a, glu, glu, o, wg, pw, pb, ps, cw, cb, lg, lb, woa, wob, woc, wo)


def _moe_kernel(h_ref, g_ref, wr2_ref, rb_ref, w1_ref, w3_ref, w2_ref, out_ref, v_ref, comb_ref):
    c = pl.program_id(1)

    @pl.when(c == 0)
    def _():
        vn = _rms_norm(h_ref[...], g_ref[...])
        v_hi, v_lo = _split_bf16(vn)
        v_ref[...] = v_hi
        wr2 = wr2_ref[...]
        r2 = _dot(v_hi, wr2)
        logits = r2[:, :LANES] + r2[:, LANES:] + _dot(v_lo, wr2[:, :LANES]) + rb_ref[...]
        lt = logits.T
        gl = [lt[g:g + 1, :] for g in range(N_GROUPS)]
        gmax = functools.reduce(jnp.maximum, gl)
        gsum = functools.reduce(lambda x, y: x + y, [jnp.exp(x - gmax) for x in gl])
        gw = 1.0 / gsum
        sel, taken = [], jnp.zeros_like(gmax, dtype=jnp.bool_)
        for g in range(N_GROUPS):
            s = jnp.logical_and(gl[g] == gmax, jnp.logical_not(taken))
            sel.append(s)
            taken = jnp.logical_or(taken, s)
        el = []
        for j in range(EPG):
            e = jnp.zeros_like(gmax)
            for g in range(N_GROUPS):
                e = jnp.where(sel[g], lt[N_GROUPS + g * EPG + j:N_GROUPS + g * EPG + j + 1, :], e)
            el.append(e)
        neg = jnp.full_like(gmax, -jnp.inf)
        top1 = functools.reduce(jnp.maximum, el)
        is1, taken = [], jnp.zeros_like(gmax, dtype=jnp.bool_)
        for j in range(EPG):
            s = jnp.logical_and(el[j] == top1, jnp.logical_not(taken))
            is1.append(s)
            taken = jnp.logical_or(taken, s)
        rest = [jnp.where(is1[j], neg, el[j]) for j in range(EPG)]
        top2 = functools.reduce(jnp.maximum, rest)
        is2, taken = [], jnp.zeros_like(gmax, dtype=jnp.bool_)
        for j in range(EPG):
            s = jnp.logical_and(jnp.logical_and(rest[j] == top2, jnp.logical_not(is1[j])),
                                jnp.logical_not(taken))
            is2.append(s)
            taken = jnp.logical_or(taken, s)
        w1 = 1.0 / (1.0 + jnp.exp(top2 - top1))
        w2 = 1.0 - w1
        rows = []
        for g in range(N_GROUPS):
            for j in range(EPG):
                wj = jnp.where(is1[j], w1, jnp.where(is2[j], w2, 0.0))
                rows.append(jnp.where(sel[g], wj * gw, 0.0))
        rows.append(jnp.zeros((LANES - N_EXPERTS, TM), F32))
        comb_ref[...] = jnp.concatenate(rows, axis=0).T
        out_ref[...] = h_ref[...]

    v = v_ref[...]
    h1 = _dot(v, w1_ref[0])
    h3 = _dot(v, w3_ref[0])
    he = h1 * jax.nn.sigmoid(h1) * h3
    comb = comb_ref[...]
    lane = lax.broadcasted_iota(jnp.int32, (TM, LANES), 1)
    parts = []
    for j in range(EPG):
        wcol = jnp.sum(jnp.where(lane == c * EPG + j, comb, 0.0), axis=-1, keepdims=True)
        parts.append((he[:, j * EXPERT_HIDDEN:(j + 1) * EXPERT_HIDDEN] * wcol).astype(BF16))
    out_ref[...] += _dot(jnp.concatenate(parts, axis=1), w2_ref[0])


def _moe(h, g, wr2, rb, w1g, w3g, w2g):
    tp = h.shape[0]
    nblk = tp // TM
    const = lambda x: pl.BlockSpec(x.shape, lambda i, c: (0, 0))
    wspec = pl.BlockSpec((1, D, GROUP_HIDDEN), lambda i, c: (c, 0, 0))
    return pl.pallas_call(
        _moe_kernel,
        grid=(nblk, N_GROUPS),
        in_specs=[pl.BlockSpec((TM, D), lambda i, c: (i, 0)), const(g), const(wr2), const(rb),
                  wspec, wspec, pl.BlockSpec((1, GROUP_HIDDEN, D), lambda i, c: (c, 0, 0))],
        out_specs=pl.BlockSpec((TM, D), lambda i, c: (i, 0)),
        out_shape=jax.ShapeDtypeStruct((tp, D), F32),
        scratch_shapes=[pltpu.VMEM((TM, D), BF16), pltpu.VMEM((TM, LANES), F32)],
        compiler_params=pltpu.CompilerParams(dimension_semantics=("arbitrary", "arbitrary"),
                                             vmem_limit_bytes=VMEM_LIMIT),
        name="hier_moe",
    )(h, g, wr2, rb, w1g, w3g, w2g)


def _final_norm_kernel(h_ref, g_ref, out_ref):
    out_ref[...] = _rms_norm(h_ref[...], g_ref[...])


def _final_norm(h, g):
    tp = h.shape[0]
    return pl.pallas_call(
        _final_norm_kernel,
        grid=(tp // TM,),
        in_specs=[pl.BlockSpec((TM, D), lambda i: (i, 0)), pl.BlockSpec((1, D), lambda i: (0, 0))],
        out_specs=pl.BlockSpec((TM, D), lambda i: (i, 0)),
        out_shape=jax.ShapeDtypeStruct((tp, D), F32),
        compiler_params=pltpu.CompilerParams(dimension_semantics=("arbitrary",)),
        name="final_norm",
    )(h, g)


def _pad_lanes(x, n=LANES):
    return jnp.pad(x, ((0, 0), (0, n - x.shape[1])))


def _pad_rows(w, n):
    return jnp.pad(w, ((0, n - w.shape[0]), (0, 0)))


def _hi_lo_weight(w):
    hi = w.astype(BF16)
    lo = (w - hi.astype(F32)).astype(BF16)
    return jnp.concatenate([_pad_lanes(hi), _pad_lanes(lo)], axis=1)


def _per_head_cols(w):
    k = w.shape[0]
    w = w.reshape(k, HEADS, HEAD_DIM)
    return jnp.pad(w, ((0, 0), (0, 0), (0, HEAD_PAD - HEAD_DIM))).reshape(k, ATT_W)


def _bias_select():
    sel = jnp.zeros((LANES, 2, HEADS, HEAD_PAD), F32)
    one_lane = N_FPIECE * HEADS
    for p in range(N_FPIECE):
        for hd in range(HEADS):
            sel = sel.at[p * HEADS + hd, 0, hd, HEAD_DIM + p].set(1.0)
            sel = sel.at[p * HEADS + hd, 1, hd, HEAD_DIM + N_FPIECE + p].set(-1.0)
    sel = sel.at[one_lane, 0, :, HEAD_DIM + N_FPIECE:HEAD_DIM + 2 * N_FPIECE].set(1.0)
    sel = sel.at[one_lane, 1, :, HEAD_DIM:HEAD_DIM + N_FPIECE].set(1.0)
    return sel.reshape(LANES, 2 * ATT_W).astype(BF16)


def kernel(x, meta, norm1_g, w_in, b_forget, pool_w, pool_b, pool_scale, conv_w, conv_b, conv_ln_g,
           conv_ln_b, w_out_a, w_out_b, w_out_c, w_o, norm2_g, router_g, router_g_b, router_e,
           router_e_b, exp_w1, exp_w3, exp_w2, final_g):
    B, L, _ = x.shape
    depth = w_in.shape[0]
    seq = N_META + L
    assert seq <= LP and LP % TM == 0 and LP % TQ == 0
    m = jnp.broadcast_to(meta.astype(x.dtype)[None], (B, N_META, D))
    h = jnp.concatenate([m, x, jnp.zeros((B, LP - seq, D), x.dtype)], axis=1).reshape(B * LP, D)
    tri = jnp.tril(jnp.ones((TM, TM), BF16))
    sel = _bias_select()

    c_abc = POOL_W + 2 * CONV_W
    c_q = c_abc + FOX_W
    c_k = c_q + FOX_W
    c_v = c_k + FOX_W
    c_f = c_v + HEADS
    for l in range(depth):
        wl = w_in[l]
        wabc = wl[:, :c_abc].astype(BF16)
        wq = _per_head_cols(wl[:, c_abc:c_q] * (HEAD_DIM ** -0.5 * LOG2E))
        wk = _per_head_cols(wl[:, c_q:c_k])
        wqk = jnp.concatenate([wq, wk], axis=1).astype(BF16)
        wvt = _per_head_cols(wl[:, c_k:c_v]).T.astype(BF16)
        wf2 = _hi_lo_weight(wl[:, c_v:c_f])
        wg = wl[:, c_f:].astype(BF16)
        bf = _pad_lanes(b_forget[l][None, :])
        a, glu, q, k, vt = _in_proj(h, norm1_g[l][None, :], wabc, wqk, wvt, wf2, bf, tri, sel, B)
        o = _attention(q, k, vt, B)

        pw = jax.scipy.linalg.block_diag(*[pool_w[l, g] for g in range(pool_w.shape[1])]).astype(BF16)
        h = _mixer_out(h, norm1_g[l][None, :], a, glu, o, wg, pw, pool_b[l].reshape(1, POOL_W),
                       pool_scale[l][None, :], _pad_rows(conv_w[l], 32), conv_b[l][None, :],
                       conv_ln_g[l][None, :], conv_ln_b[l][None, :],
                       w_out_a[l].astype(BF16), w_out_b[l].astype(BF16), w_out_c[l].astype(BF16),
                       w_o[l].astype(BF16))

        wr2 = _hi_lo_weight(jnp.concatenate([router_g[l], router_e[l]], axis=1))
        rb = _pad_lanes(jnp.concatenate([router_g_b[l], router_e_b[l]])[None, :])
        w1g = exp_w1[l].reshape(N_GROUPS, EPG, D, EXPERT_HIDDEN).transpose(0, 2, 1, 3).reshape(N_GROUPS, D, GROUP_HIDDEN)
        w3g = exp_w3[l].reshape(N_GROUPS, EPG, D, EXPERT_HIDDEN).transpose(0, 2, 1, 3).reshape(N_GROUPS, D, GROUP_HIDDEN)
        w2g = exp_w2[l].reshape(N_GROUPS, GROUP_HIDDEN, D)
        h = _moe(h, norm2_g[l][None, :], wr2, rb, w1g.astype(BF16), w3g.astype(BF16), w2g.astype(BF16))

    out = _final_norm(h, final_g[None, :])
    return out.reshape(B, LP, D)[:, N_META:seq]
```

```python
import functools
import math

import jax
import jax.numpy as jnp
from jax import lax
from jax.experimental import pallas as pl
from jax.experimental.pallas import tpu as pltpu

D = 1024
N_META = 16
EPS = 1e-6
POOL_W = 256
POOL_WINDOWS = (2, 4, 8, 16)
CONV_W = 256
CONV_K = 31
HEADS = 8
HEAD_DIM = 64
FOX_W = HEADS * HEAD_DIM
N_GROUPS = 4
EPG = 4
N_EXPERTS = 16
EXPERT_HIDDEN = 256
GROUP_HIDDEN = EPG * EXPERT_HIDDEN

LANES = 128
HEAD_PAD = LANES
ATT_W = HEADS * HEAD_PAD
LP = 8704
TM = 512
BLOCKS_PER_SEQ = LP // TM
HALO = 32
TQ = 512
N_FPIECE = 3
VMEM_LIMIT = 56 * 1024 * 1024
LOG2E = math.log2(math.e)

F32 = jnp.float32
BF16 = jnp.bfloat16


def _dot(a, b):
    return jnp.dot(a, b, preferred_element_type=F32)


def _dot_nt(a, b):
    return lax.dot_general(a, b, (((1,), (1,)), ((), ())), preferred_element_type=F32)


def _rms_norm(x, g):
    ms = jnp.mean(x * x, axis=-1, keepdims=True)
    return x * lax.rsqrt(ms + EPS) * g


def _split_bf16(x):
    hi = x.astype(BF16)
    lo = (x - hi.astype(F32)).astype(BF16)
    return hi, lo


def _in_proj_kernel(h_ref, g_ref, wabc_ref, wqk_ref, wvt_ref, wf_ref, bf_ref, tri_ref, sel_ref,
                    a_ref, glu_ref, q_ref, k_ref, vt_ref, carry_ref):
    i = pl.program_id(0)

    @pl.when(i % BLOCKS_PER_SEQ == 0)
    def _():
        carry_ref[...] = jnp.zeros_like(carry_ref)

    u = _rms_norm(h_ref[...], g_ref[...]).astype(BF16)

    abc = _dot(u, wabc_ref[...])
    a_ref[...] = abc[:, :POOL_W].astype(BF16)
    glu = abc[:, POOL_W:POOL_W + CONV_W] * jax.nn.sigmoid(abc[:, POOL_W + CONV_W:])
    glu_ref[...] = glu.astype(BF16)

    fg = _dot(u, wf_ref[...]) + bf_ref[...]
    logf = jnp.minimum(fg, 0.0) - jnp.log1p(jnp.exp(-jnp.abs(fg)))
    l_hi, l_lo = _split_bf16(logf)
    cum2 = _dot(tri_ref[...], jnp.concatenate([l_hi, l_lo], axis=1))
    cum = cum2[:, :LANES] + cum2[:, LANES:] + carry_ref[...]
    carry_ref[...] = cum[TM - 1:TM, :]

    lane = lax.broadcasted_iota(jnp.int32, (TM, LANES), 1)
    fv = jnp.where(lane < HEADS, cum * LOG2E, 0.0)
    f_hi = fv.astype(BF16).astype(F32)
    r1 = fv - f_hi
    f_mid = r1.astype(BF16).astype(F32)
    f_lo = (r1 - f_mid).astype(BF16).astype(F32)
    fcat = f_hi + pltpu.roll(f_mid, HEADS, 1) + pltpu.roll(f_lo, 2 * HEADS, 1)
    fcat = jnp.where(lane == N_FPIECE * HEADS, 1.0, fcat).astype(BF16)

    qk = _dot(u, wqk_ref[...]) + _dot(fcat, sel_ref[...])
    q_ref[...] = qk[:, :ATT_W].astype(BF16)
    k_ref[...] = qk[:, ATT_W:].astype(BF16)

    vt = _dot_nt(wvt_ref[...], u)
    row = lax.broadcasted_iota(jnp.int32, (ATT_W, TM), 0)
    vt_ref[0] = jnp.where(row % HEAD_PAD == HEAD_DIM, 1.0, vt).astype(BF16)


def _in_proj(h, g, wabc, wqk, wvt, wf, bf, tri, sel, batch):
    tp = h.shape[0]
    nblk = tp // TM
    row = lambda w: pl.BlockSpec((TM, w), lambda i: (i, 0))
    const = lambda x: pl.BlockSpec(x.shape, lambda i: (0, 0))
    return pl.pallas_call(
        _in_proj_kernel,
        grid=(nblk,),
        in_specs=[row(D), const(g), const(wabc), const(wqk), const(wvt), const(wf), const(bf),
                  const(tri), const(sel)],
        out_specs=[row(POOL_W), row(CONV_W), row(ATT_W), row(ATT_W),
                   pl.BlockSpec((1, ATT_W, TM), lambda i: (i // BLOCKS_PER_SEQ, 0, i % BLOCKS_PER_SEQ))],
        out_shape=[jax.ShapeDtypeStruct((tp, POOL_W), BF16), jax.ShapeDtypeStruct((tp, CONV_W), BF16),
                   jax.ShapeDtypeStruct((tp, ATT_W), BF16), jax.ShapeDtypeStruct((tp, ATT_W), BF16),
                   jax.ShapeDtypeStruct((batch, ATT_W, LP), BF16)],
        scratch_shapes=[pltpu.VMEM((1, LANES), F32)],
        compiler_params=pltpu.CompilerParams(dimension_semantics=("arbitrary",),
                                             vmem_limit_bytes=VMEM_LIMIT),
        name="in_proj",
    )(h, g, wabc, wqk, wvt, wf, bf, tri, sel)


HEADS_PER_STEP = 4


def _attn_kernel(q_ref, k_ref, vt_ref, o_ref, m_ref, acc_ref, s_ref, mt_ref):
    qi = pl.program_id(2)
    m_ref[...] = jnp.full_like(m_ref, -jnp.inf)
    acc_ref[...] = jnp.zeros_like(acc_ref)

    def scores(j, slot, masked=False):
        start = pl.multiple_of(j * TQ, TQ)
        for hd in range(HEADS_PER_STEP):
            cols = slice(hd * HEAD_PAD, (hd + 1) * HEAD_PAD)
            s = _dot_nt(k_ref[pl.ds(start, TQ), cols], q_ref[:, cols])
            if masked:
                key = lax.broadcasted_iota(jnp.int32, (TQ, TQ), 0)
                qry = lax.broadcasted_iota(jnp.int32, (TQ, TQ), 1)
                s = jnp.where(key <= qry, s, -jnp.inf)
            s_ref[slot, hd] = s
            mt_ref[slot, hd] = jnp.max(s, axis=0, keepdims=True)

    def values(j, slot):
        start = pl.multiple_of(j * TQ, TQ)
        for hd in range(HEADS_PER_STEP):
            cols = slice(hd * HEAD_PAD, (hd + 1) * HEAD_PAD)
            m_old = m_ref[hd]
            m_new = jnp.maximum(m_old, mt_ref[slot, hd])
            p = jnp.exp2(s_ref[slot, hd] - m_new).astype(BF16)
            alpha = jnp.exp2(m_old - m_new)
            pv = _dot(vt_ref[0, cols, pl.ds(start, TQ)], p)
            acc_ref[hd] = alpha * acc_ref[hd] + pv
            m_ref[hd] = m_new

    @pl.when(qi == 0)
    def _():
        scores(0, 0, masked=True)
        values(0, 0)

    @pl.when(qi > 0)
    def _():
        scores(0, 0)

        def body(t, c):
            j = 2 * t
            scores(j + 1, 1)
            values(j, 0)
            scores(j + 2, 0)
            values(j + 1, 1)
            return c

        last = qi - 1
        lax.fori_loop(0, last // 2, body, 0)

        @pl.when(last % 2 == 0)
        def _():
            scores(qi, 1, masked=True)
            values(last, 0)
            values(qi, 1)

        @pl.when(last % 2 == 1)
        def _():
            scores(last, 1)
            values(last - 1, 0)
            scores(qi, 0, masked=True)
            values(last, 1)
            values(qi, 0)

    outs = []
    for hd in range(HEADS_PER_STEP):
        acc = acc_ref[hd]
        outs.append(acc[:HEAD_DIM, :] / acc[HEAD_DIM:HEAD_DIM + 1, :])
    o_ref[...] = jnp.concatenate(outs, axis=0).T.astype(BF16)


def _attention(q, k, vt, batch):
    nq = LP // TQ
    w = HEADS_PER_STEP * HEAD_PAD
    return pl.pallas_call(
        _attn_kernel,
        grid=(batch, HEADS // HEADS_PER_STEP, nq),
        in_specs=[pl.BlockSpec((TQ, w), lambda b, h, i: (b * nq + i, h)),
                  pl.BlockSpec((LP, w), lambda b, h, i: (b, h)),
                  pl.BlockSpec((1, w, LP), lambda b, h, i: (b, h, 0))],
        out_specs=pl.BlockSpec((TQ, HEADS_PER_STEP * HEAD_DIM), lambda b, h, i: (b * nq + i, h)),
        out_shape=jax.ShapeDtypeStruct((batch * LP, FOX_W), BF16),
        scratch_shapes=[pltpu.VMEM((HEADS_PER_STEP, 1, TQ), F32),
                        pltpu.VMEM((HEADS_PER_STEP, HEAD_PAD, TQ), F32),
                        pltpu.VMEM((2, HEADS_PER_STEP, TQ, TQ), F32),
                        pltpu.VMEM((2, HEADS_PER_STEP, 1, TQ), F32)],
        compiler_params=pltpu.CompilerParams(dimension_semantics=("arbitrary", "arbitrary", "arbitrary"),
                                             vmem_limit_bytes=VMEM_LIMIT),
        name="fox_attention",
    )(q, k, vt)


CONV_CHUNK = 64
GATE_CHUNK = 256
PH_ROWS = TM + HALO - 8


def _mixer_out_kernel(h_ref, g_ref, a_ref, ah_ref, glu_ref, gh_ref, o_ref, wg_ref,
                      pw_ref, pb_ref, ps_ref, cw_ref, cb_ref, lg_ref, lb_ref,
                      woa_ref, wob_ref, woc_ref, wo_ref, out_ref, xa_ref, xg_ref, ph_ref, yb_ref, gate_ref):
    i = pl.program_id(0)
    blk = i % BLOCKS_PER_SEQ
    keep = (blk > 0).astype(F32)

    h = h_ref[...]
    u = _rms_norm(h, g_ref[...]).astype(BF16)

    xa_ref[0:HALO, :] = ah_ref[...].astype(F32) * keep
    xa_ref[HALO:, :] = a_ref[...].astype(F32)
    xg_ref[0:HALO, :] = gh_ref[...].astype(F32) * keep
    xg_ref[HALO:, :] = glu_ref[...].astype(F32)
    for r in range(1, 8):
        ph_ref[r - 1] = xg_ref[r:r + PH_ROWS, :]

    cw = cw_ref[...]
    base = HALO - (CONV_K - 1)

    def conv_rows(c0):
        acc = jnp.zeros((CONV_CHUNK, CONV_W), F32) + cb_ref[...]
        for j in range(CONV_K):
            off = base + j
            r = off % 8
            rows = pl.ds(c0 + (off - r), CONV_CHUNK)
            tap = xg_ref[rows, :] if r == 0 else ph_ref[r - 1, rows, :]
            acc = acc + tap * cw[j:j + 1, :]
        mu = jnp.mean(acc, axis=-1, keepdims=True)
        cen = acc - mu
        var = jnp.mean(cen * cen, axis=-1, keepdims=True)
        y = cen * lax.rsqrt(var + EPS) * lg_ref[...] + lb_ref[...]
        yb_ref[pl.ds(c0, CONV_CHUNK), :] = (y * jax.nn.sigmoid(y)).astype(BF16)

    for n in range(TM // CONV_CHUNK):
        conv_rows(n * CONV_CHUNK)
    for n in range(3 * D // GATE_CHUNK):
        cols = slice(n * GATE_CHUNK, (n + 1) * GATE_CHUNK)
        gate_ref[:, cols] = jax.nn.sigmoid(_dot(u, wg_ref[:, cols]))
    merged = gate_ref[:, 2 * D:3 * D] * _dot(o_ref[...], woc_ref[...])

    pos = blk * TM + lax.broadcasted_iota(jnp.int32, (TM, LANES), 0) + 1
    lane = lax.broadcasted_iota(jnp.int32, (TM, LANES), 1)
    low = lane < (LANES // 2)
    means = []
    for t in range(2):
        x = xa_ref[:, t * LANES:(t + 1) * LANES]
        s2 = x + pltpu.roll(x, 1, 0)
        s4 = s2 + pltpu.roll(s2, 2, 0)
        if t == 0:
            small, big = s2, s4
        else:
            s8 = s4 + pltpu.roll(s4, 4, 0)
            small, big = s8, s8 + pltpu.roll(s8, 8, 0)
        w_small, w_big = POOL_WINDOWS[2 * t], POOL_WINDOWS[2 * t + 1]
        cnt = jnp.minimum(pos, jnp.where(low, w_small, w_big)).astype(F32)
        means.append(jnp.where(low, small[HALO:], big[HALO:]) / cnt - x[HALO:])
    diff = jnp.concatenate(means, axis=1).astype(BF16)
    ya = (_dot(diff, pw_ref[...]) + pb_ref[...]) * ps_ref[...]
    ya = _dot(ya.astype(BF16), woa_ref[...])

    merged = merged + gate_ref[:, 0:D] * ya
    merged = merged + gate_ref[:, D:2 * D] * _dot(yb_ref[...], wob_ref[...])
    out_ref[...] = h + _dot(merged.astype(BF16), wo_ref[...])


def _mixer_out(h, g, a, glu, o, wg, pw, pb, ps, cw, cb, lg, lb, woa, wob, woc, wo):
    tp = h.shape[0]
    nblk = tp // TM
    row = lambda w: pl.BlockSpec((TM, w), lambda i: (i, 0))
    halo = lambda w: pl.BlockSpec((HALO, w), lambda i: (jnp.maximum(i * (TM // HALO) - 1, 0), 0))
    const = lambda x: pl.BlockSpec(x.shape, lambda i: (0, 0))
    return pl.pallas_call(
        _mixer_out_kernel,
        grid=(nblk,),
        in_specs=[row(D), const(g), row(POOL_W), halo(POOL_W), row(CONV_W), halo(CONV_W), row(FOX_W),
                  const(wg), const(pw), const(pb), const(ps), const(cw), const(cb), const(lg), const(lb),
                  const(woa), const(wob), const(woc), const(wo)],
        out_specs=row(D),
        out_shape=jax.ShapeDtypeStruct((tp, D), F32),
        scratch_shapes=[pltpu.VMEM((TM + HALO, POOL_W), F32), pltpu.VMEM((TM + HALO, CONV_W), F32),
                        pltpu.VMEM((7, PH_ROWS, CONV_W), F32), pltpu.VMEM((TM, CONV_W), BF16),
                        pltpu.VMEM((TM, 3 * D), F32)],
        compiler_params=pltpu.CompilerParams(dimension_semantics=("arbitrary",),
                                             vmem_limit_bytes=VMEM_LIMIT),
        name="mixer_out",
    )(h, g, a, a, glu, glu, o, wg, pw, pb, ps, cw, cb, lg, lb, woa, wob, woc, wo)


def _moe_kernel(h_ref, g_ref, wr2_ref, rb_ref, w1_ref, w3_ref, w2_ref, out_ref, v_ref, comb_ref):
    c = pl.program_id(1)

    @pl.when(c == 0)
    def _():
        vn = _rms_norm(h_ref[...], g_ref[...])
        v_hi, v_lo = _split_bf16(vn)
        v_ref[...] = v_hi
        wr2 = wr2_ref[...]
        r2 = _dot(v_hi, wr2)
        logits = r2[:, :LANES] + r2[:, LANES:] + _dot(v_lo, wr2[:, :LANES]) + rb_ref[...]
        lt = logits.T
        gl = [lt[g:g + 1, :] for g in range(N_GROUPS)]
        gmax = functools.reduce(jnp.maximum, gl)
        gsum = functools.reduce(lambda x, y: x + y, [jnp.exp(x - gmax) for x in gl])
        gw = 1.0 / gsum
        sel, taken = [], jnp.zeros_like(gmax, dtype=jnp.bool_)
        for g in range(N_GROUPS):
            s = jnp.logical_and(gl[g] == gmax, jnp.logical_not(taken))
            sel.append(s)
            taken = jnp.logical_or(taken, s)
        el = []
        for j in range(EPG):
            e = jnp.zeros_like(gmax)
            for g in range(N_GROUPS):
                e = jnp.where(sel[g], lt[N_GROUPS + g * EPG + j:N_GROUPS + g * EPG + j + 1, :], e)
            el.append(e)
        neg = jnp.full_like(gmax, -jnp.inf)
        top1 = functools.reduce(jnp.maximum, el)
        is1, taken = [], jnp.zeros_like(gmax, dtype=jnp.bool_)
        for j in range(EPG):
            s = jnp.logical_and(el[j] == top1, jnp.logical_not(taken))
            is1.append(s)
            taken = jnp.logical_or(taken, s)
        rest = [jnp.where(is1[j], neg, el[j]) for j in range(EPG)]
        top2 = functools.reduce(jnp.maximum, rest)
        is2, taken = [], jnp.zeros_like(gmax, dtype=jnp.bool_)
        for j in range(EPG):
            s = jnp.logical_and(jnp.logical_and(rest[j] == top2, jnp.logical_not(is1[j])),
                                jnp.logical_not(taken))
            is2.append(s)
            taken = jnp.logical_or(taken, s)
        w1 = 1.0 / (1.0 + jnp.exp(top2 - top1))
        w2 = 1.0 - w1
        rows = []
        for g in range(N_GROUPS):
            for j in range(EPG):
                wj = jnp.where(is1[j], w1, jnp.where(is2[j], w2, 0.0))
                rows.append(jnp.where(sel[g], wj * gw, 0.0))
        rows.append(jnp.zeros((LANES - N_EXPERTS, TM), F32))
        comb_ref[...] = jnp.concatenate(rows, axis=0).T
        out_ref[...] = h_ref[...]

    v = v_ref[...]
    h1 = _dot(v, w1_ref[0])
    h3 = _dot(v, w3_ref[0])
    he = h1 * jax.nn.sigmoid(h1) * h3
    comb = comb_ref[...]
    lane = lax.broadcasted_iota(jnp.int32, (TM, LANES), 1)
    parts = []
    for j in range(EPG):
        wcol = jnp.sum(jnp.where(lane == c * EPG + j, comb, 0.0), axis=-1, keepdims=True)
        parts.append((he[:, j * EXPERT_HIDDEN:(j + 1) * EXPERT_HIDDEN] * wcol).astype(BF16))
    out_ref[...] += _dot(jnp.concatenate(parts, axis=1), w2_ref[0])


def _moe(h, g, wr2, rb, w1g, w3g, w2g):
    tp = h.shape[0]
    nblk = tp // TM
    const = lambda x: pl.BlockSpec(x.shape, lambda i, c: (0, 0))
    wspec = pl.BlockSpec((1, D, GROUP_HIDDEN), lambda i, c: (c, 0, 0))
    return pl.pallas_call(
        _moe_kernel,
        grid=(nblk, N_GROUPS),
        in_specs=[pl.BlockSpec((TM, D), lambda i, c: (i, 0)), const(g), const(wr2), const(rb),
                  wspec, wspec, pl.BlockSpec((1, GROUP_HIDDEN, D), lambda i, c: (c, 0, 0))],
        out_specs=pl.BlockSpec((TM, D), lambda i, c: (i, 0)),
        out_shape=jax.ShapeDtypeStruct((tp, D), F32),
        scratch_shapes=[pltpu.VMEM((TM, D), BF16), pltpu.VMEM((TM, LANES), F32)],
        compiler_params=pltpu.CompilerParams(dimension_semantics=("arbitrary", "arbitrary"),
                                             vmem_limit_bytes=VMEM_LIMIT),
        name="hier_moe",
    )(h, g, wr2, rb, w1g, w3g, w2g)


def _final_norm_kernel(h_ref, g_ref, out_ref):
    out_ref[...] = _rms_norm(h_ref[...], g_ref[...])


def _final_norm(h, g, batch, seq_out):
    nblk = seq_out // TM
    return pl.pallas_call(
        _final_norm_kernel,
        grid=(batch, nblk),
        in_specs=[pl.BlockSpec((pl.Element(TM), pl.Element(D)),
                               lambda b, r: (pl.multiple_of(b * LP + N_META + r * TM, N_META), 0)),
                  pl.BlockSpec((1, D), lambda b, r: (0, 0))],
        out_specs=pl.BlockSpec((TM, D), lambda b, r: (b * nblk + r, 0)),
        out_shape=jax.ShapeDtypeStruct((batch * seq_out, D), F32),
        compiler_params=pltpu.CompilerParams(dimension_semantics=("arbitrary", "arbitrary")),
        name="final_norm",
    )(h, g)


def _pad_lanes(x, n=LANES):
    return jnp.pad(x, ((0, 0), (0, n - x.shape[1])))


def _pad_rows(w, n):
    return jnp.pad(w, ((0, n - w.shape[0]), (0, 0)))


def _hi_lo_weight(w):
    hi = w.astype(BF16)
    lo = (w - hi.astype(F32)).astype(BF16)
    return jnp.concatenate([_pad_lanes(hi), _pad_lanes(lo)], axis=1)


def _per_head_cols(w):
    k = w.shape[0]
    w = w.reshape(k, HEADS, HEAD_DIM)
    return jnp.pad(w, ((0, 0), (0, 0), (0, HEAD_PAD - HEAD_DIM))).reshape(k, ATT_W)


def _bias_select():
    sel = jnp.zeros((LANES, 2, HEADS, HEAD_PAD), F32)
    one_lane = N_FPIECE * HEADS
    for p in range(N_FPIECE):
        for hd in range(HEADS):
            sel = sel.at[p * HEADS + hd, 0, hd, HEAD_DIM + p].set(1.0)
            sel = sel.at[p * HEADS + hd, 1, hd, HEAD_DIM + N_FPIECE + p].set(-1.0)
    sel = sel.at[one_lane, 0, :, HEAD_DIM + N_FPIECE:HEAD_DIM + 2 * N_FPIECE].set(1.0)
    sel = sel.at[one_lane, 1, :, HEAD_DIM:HEAD_DIM + N_FPIECE].set(1.0)
    return sel.reshape(LANES, 2 * ATT_W).astype(BF16)


def kernel(x, meta, norm1_g, w_in, b_forget, pool_w, pool_b, pool_scale, conv_w, conv_b, conv_ln_g,
           conv_ln_b, w_out_a, w_out_b, w_out_c, w_o, norm2_g, router_g, router_g_b, router_e,
           router_e_b, exp_w1, exp_w3, exp_w2, final_g):
    B, L, _ = x.shape
    depth = w_in.shape[0]
    seq = N_META + L
    assert seq <= LP and LP % TM == 0 and LP % TQ == 0
    m = jnp.broadcast_to(meta.astype(x.dtype)[None], (B, N_META, D))
    h = jnp.concatenate([m, x, jnp.zeros((B, LP - seq, D), x.dtype)], axis=1).reshape(B * LP, D)
    tri = jnp.tril(jnp.ones((TM, TM), BF16))
    sel = _bias_select()

    c_abc = POOL_W + 2 * CONV_W
    c_q = c_abc + FOX_W
    c_k = c_q + FOX_W
    c_v = c_k + FOX_W
    c_f = c_v + HEADS
    for l in range(depth):
        wl = w_in[l]
        wabc = wl[:, :c_abc].astype(BF16)
        wq = _per_head_cols(wl[:, c_abc:c_q] * (HEAD_DIM ** -0.5 * LOG2E))
        wk = _per_head_cols(wl[:, c_q:c_k])
        wqk = jnp.concatenate([wq, wk], axis=1).astype(BF16)
        wvt = _per_head_cols(wl[:, c_k:c_v]).T.astype(BF16)
        wf = _pad_lanes(wl[:, c_v:c_f]).astype(BF16)
        wg = wl[:, c_f:].astype(BF16)
        bf = _pad_lanes(b_forget[l][None, :])
        a, glu, q, k, vt = _in_proj(h, norm1_g[l][None, :], wabc, wqk, wvt, wf, bf, tri, sel, B)
        o = _attention(q, k, vt, B)

        pw = jax.scipy.linalg.block_diag(*[pool_w[l, g] for g in range(pool_w.shape[1])]).astype(BF16)
        h = _mixer_out(h, norm1_g[l][None, :], a, glu, o, wg, pw, pool_b[l].reshape(1, POOL_W),
                       pool_scale[l][None, :], _pad_rows(conv_w[l], 32), conv_b[l][None, :],
                       conv_ln_g[l][None, :], conv_ln_b[l][None, :],
                       w_out_a[l].astype(BF16), w_out_b[l].astype(BF16), w_out_c[l].astype(BF16),
                       w_o[l].astype(BF16))

        wr2 = _hi_lo_weight(jnp.concatenate([router_g[l], router_e[l]], axis=1))
        rb = _pad_lanes(jnp.concatenate([router_g_b[l], router_e_b[l]])[None, :])
        w1g = exp_w1[l].reshape(N_GROUPS, EPG, D, EXPERT_HIDDEN).transpose(0, 2, 1, 3).reshape(N_GROUPS, D, GROUP_HIDDEN)
        w3g = exp_w3[l].reshape(N_GROUPS, EPG, D, EXPERT_HIDDEN).transpose(0, 2, 1, 3).reshape(N_GROUPS, D, GROUP_HIDDEN)
        w2g = exp_w2[l].reshape(N_GROUPS, GROUP_HIDDEN, D)
        h = _moe(h, norm2_g[l][None, :], wr2, rb, w1g.astype(BF16), w3g.astype(BF16), w2g.astype(BF16))

    assert L % TM == 0
    return _final_norm(h, final_g[None, :], B, L).reshape(B, L, D)
```

```python
import functools
import math

import jax
import jax.numpy as jnp
from jax import lax
from jax.experimental import pallas as pl
from jax.experimental.pallas import tpu as pltpu

D = 1024
N_META = 16
EPS = 1e-6
POOL_W = 256
POOL_WINDOWS = (2, 4, 8, 16)
CONV_W = 256
CONV_K = 31
HEADS = 8
HEAD_DIM = 64
FOX_W = HEADS * HEAD_DIM
N_GROUPS = 4
EPG = 4
N_EXPERTS = 16
EXPERT_HIDDEN = 256
GROUP_HIDDEN = EPG * EXPERT_HIDDEN

LANES = 128
HEAD_PAD = LANES
ATT_W = HEADS * HEAD_PAD
LP = 8704
TM = 512
BLOCKS_PER_SEQ = LP // TM
HALO = 32
TQ = 512
N_FPIECE = 3
VMEM_LIMIT = 56 * 1024 * 1024
MOE_VMEM_LIMIT = 60 * 1024 * 1024
LOG2E = math.log2(math.e)

F32 = jnp.float32
BF16 = jnp.bfloat16


def _dot(a, b):
    return jnp.dot(a, b, preferred_element_type=F32)


def _dot_nt(a, b):
    return lax.dot_general(a, b, (((1,), (1,)), ((), ())), preferred_element_type=F32)


def _rms_norm(x, g):
    ms = jnp.mean(x * x, axis=-1, keepdims=True)
    return x * lax.rsqrt(ms + EPS) * g


def _split_bf16(x):
    hi = x.astype(BF16)
    lo = (x - hi.astype(F32)).astype(BF16)
    return hi, lo


def _in_proj_kernel(h_ref, g_ref, wabc_ref, wqk_ref, wvt_ref, wf_ref, bf_ref, tri_ref, sel_ref,
                    a_ref, glu_ref, q_ref, k_ref, vt_ref, carry_ref):
    i = pl.program_id(0)

    @pl.when(i % BLOCKS_PER_SEQ == 0)
    def _():
        carry_ref[...] = jnp.zeros_like(carry_ref)

    u = _rms_norm(h_ref[...], g_ref[...]).astype(BF16)

    abc = _dot(u, wabc_ref[...])
    a_ref[...] = abc[:, :POOL_W].astype(BF16)
    glu = abc[:, POOL_W:POOL_W + CONV_W] * jax.nn.sigmoid(abc[:, POOL_W + CONV_W:])
    glu_ref[...] = glu.astype(BF16)

    fg = _dot(u, wf_ref[...]) + bf_ref[...]
    logf = jnp.minimum(fg, 0.0) - jnp.log1p(jnp.exp(-jnp.abs(fg)))
    l_hi, l_lo = _split_bf16(logf)
    cum2 = _dot(tri_ref[...], jnp.concatenate([l_hi, l_lo], axis=1))
    cum = cum2[:, :LANES] + cum2[:, LANES:] + carry_ref[...]
    carry_ref[...] = cum[TM - 1:TM, :]

    lane = lax.broadcasted_iota(jnp.int32, (TM, LANES), 1)
    fv = jnp.where(lane < HEADS, cum * LOG2E, 0.0)
    f_hi = fv.astype(BF16).astype(F32)
    r1 = fv - f_hi
    f_mid = r1.astype(BF16).astype(F32)
    f_lo = (r1 - f_mid).astype(BF16).astype(F32)
    fcat = f_hi + pltpu.roll(f_mid, HEADS, 1) + pltpu.roll(f_lo, 2 * HEADS, 1)
    fcat = jnp.where(lane == N_FPIECE * HEADS, 1.0, fcat).astype(BF16)

    qk = _dot(u, wqk_ref[...]) + _dot(fcat, sel_ref[...])
    q_ref[...] = qk[:, :ATT_W].astype(BF16)
    k_ref[...] = qk[:, ATT_W:].astype(BF16)

    vt = _dot_nt(wvt_ref[...], u)
    row = lax.broadcasted_iota(jnp.int32, (ATT_W, TM), 0)
    vt_ref[0] = jnp.where(row % HEAD_PAD == HEAD_DIM, 1.0, vt).astype(BF16)


def _in_proj(h, g, wabc, wqk, wvt, wf, bf, tri, sel, batch):
    tp = h.shape[0]
    nblk = tp // TM
    row = lambda w: pl.BlockSpec((TM, w), lambda i: (i, 0))
    const = lambda x: pl.BlockSpec(x.shape, lambda i: (0, 0))
    return pl.pallas_call(
        _in_proj_kernel,
        grid=(nblk,),
        in_specs=[row(D), const(g), const(wabc), const(wqk), const(wvt), const(wf), const(bf),
                  const(tri), const(sel)],
        out_specs=[row(POOL_W), row(CONV_W), row(ATT_W), row(ATT_W),
                   pl.BlockSpec((1, ATT_W, TM), lambda i: (i // BLOCKS_PER_SEQ, 0, i % BLOCKS_PER_SEQ))],
        out_shape=[jax.ShapeDtypeStruct((tp, POOL_W), BF16), jax.ShapeDtypeStruct((tp, CONV_W), BF16),
                   jax.ShapeDtypeStruct((tp, ATT_W), BF16), jax.ShapeDtypeStruct((tp, ATT_W), BF16),
                   jax.ShapeDtypeStruct((batch, ATT_W, LP), BF16)],
        scratch_shapes=[pltpu.VMEM((1, LANES), F32)],
        compiler_params=pltpu.CompilerParams(dimension_semantics=("arbitrary",),
                                             vmem_limit_bytes=VMEM_LIMIT),
        name="in_proj",
    )(h, g, wabc, wqk, wvt, wf, bf, tri, sel)


HEADS_PER_STEP = 4


def _attn_kernel(q_ref, k_ref, vt_ref, o_ref, m_ref, acc_ref, s_ref, mt_ref):
    qi = pl.program_id(2)
    m_ref[...] = jnp.full_like(m_ref, -jnp.inf)
    acc_ref[...] = jnp.zeros_like(acc_ref)

    def scores(j, slot, masked=False):
        start = pl.multiple_of(j * TQ, TQ)
        for hd in range(HEADS_PER_STEP):
            cols = slice(hd * HEAD_PAD, (hd + 1) * HEAD_PAD)
            s = _dot_nt(k_ref[pl.ds(start, TQ), cols], q_ref[:, cols])
            if masked:
                key = lax.broadcasted_iota(jnp.int32, (TQ, TQ), 0)
                qry = lax.broadcasted_iota(jnp.int32, (TQ, TQ), 1)
                s = jnp.where(key <= qry, s, -jnp.inf)
            s_ref[slot, hd] = s
            mt_ref[slot, hd] = jnp.max(s, axis=0, keepdims=True)

    def values(j, slot):
        start = pl.multiple_of(j * TQ, TQ)
        for hd in range(HEADS_PER_STEP):
            cols = slice(hd * HEAD_PAD, (hd + 1) * HEAD_PAD)
            m_old = m_ref[hd]
            m_new = jnp.maximum(m_old, mt_ref[slot, hd])
            p = jnp.exp2(s_ref[slot, hd] - m_new).astype(BF16)
            alpha = jnp.exp2(m_old - m_new)
            pv = _dot(vt_ref[0, cols, pl.ds(start, TQ)], p)
            acc_ref[hd] = alpha * acc_ref[hd] + pv
            m_ref[hd] = m_new

    @pl.when(qi == 0)
    def _():
        scores(0, 0, masked=True)
        values(0, 0)

    @pl.when(qi > 0)
    def _():
        scores(0, 0)

        def body(t, c):
            j = 2 * t
            scores(j + 1, 1)
            values(j, 0)
            scores(j + 2, 0)
            values(j + 1, 1)
            return c

        last = qi - 1
        lax.fori_loop(0, last // 2, body, 0)

        @pl.when(last % 2 == 0)
        def _():
            scores(qi, 1, masked=True)
            values(last, 0)
            values(qi, 1)

        @pl.when(last % 2 == 1)
        def _():
            scores(last, 1)
            values(last - 1, 0)
            scores(qi, 0, masked=True)
            values(last, 1)
            values(qi, 0)

    outs = []
    for hd in range(HEADS_PER_STEP):
        acc = acc_ref[hd]
        outs.append(acc[:HEAD_DIM, :] / acc[HEAD_DIM:HEAD_DIM + 1, :])
    o_ref[...] = jnp.concatenate(outs, axis=0).T.astype(BF16)


def _attention(q, k, vt, batch):
    nq = LP // TQ
    w = HEADS_PER_STEP * HEAD_PAD
    return pl.pallas_call(
        _attn_kernel,
        grid=(batch, HEADS // HEADS_PER_STEP, nq),
        in_specs=[pl.BlockSpec((TQ, w), lambda b, h, i: (b * nq + i, h)),
                  pl.BlockSpec((LP, w), lambda b, h, i: (b, h)),
                  pl.BlockSpec((1, w, LP), lambda b, h, i: (b, h, 0))],
        out_specs=pl.BlockSpec((TQ, HEADS_PER_STEP * HEAD_DIM), lambda b, h, i: (b * nq + i, h)),
        out_shape=jax.ShapeDtypeStruct((batch * LP, FOX_W), BF16),
        scratch_shapes=[pltpu.VMEM((HEADS_PER_STEP, 1, TQ), F32),
                        pltpu.VMEM((HEADS_PER_STEP, HEAD_PAD, TQ), F32),
                        pltpu.VMEM((2, HEADS_PER_STEP, TQ, TQ), F32),
                        pltpu.VMEM((2, HEADS_PER_STEP, 1, TQ), F32)],
        compiler_params=pltpu.CompilerParams(dimension_semantics=("arbitrary", "arbitrary", "arbitrary"),
                                             vmem_limit_bytes=VMEM_LIMIT),
        name="fox_attention",
    )(q, k, vt)


CONV_CHUNK = 64
GATE_CHUNK = 256
PH_ROWS = TM + HALO - 8


def _mixer_out_kernel(h_ref, g_ref, a_ref, ah_ref, glu_ref, gh_ref, o_ref, wg_ref,
                      pw_ref, pb_ref, ps_ref, cw_ref, cb_ref, lg_ref, lb_ref,
                      woa_ref, wob_ref, woc_ref, wo_ref, out_ref, xa_ref, xg_ref, ph_ref, yb_ref, gate_ref):
    i = pl.program_id(0)
    blk = i % BLOCKS_PER_SEQ
    keep = (blk > 0).astype(F32)

    h = h_ref[...]
    u = _rms_norm(h, g_ref[...]).astype(BF16)

    xa_ref[0:HALO, :] = ah_ref[...].astype(F32) * keep
    xa_ref[HALO:, :] = a_ref[...].astype(F32)
    xg_ref[0:HALO, :] = gh_ref[...].astype(F32) * keep
    xg_ref[HALO:, :] = glu_ref[...].astype(F32)
    for r in range(1, 8):
        ph_ref[r - 1] = xg_ref[r:r + PH_ROWS, :]

    cw = cw_ref[...]
    base = HALO - (CONV_K - 1)

    def conv_rows(c0):
        acc = jnp.zeros((CONV_CHUNK, CONV_W), F32) + cb_ref[...]
        for j in range(CONV_K):
            off = base + j
            r = off % 8
            rows = pl.ds(c0 + (off - r), CONV_CHUNK)
            tap = xg_ref[rows, :] if r == 0 else ph_ref[r - 1, rows, :]
            acc = acc + tap * cw[j:j + 1, :]
        mu = jnp.mean(acc, axis=-1, keepdims=True)
        cen = acc - mu
        var = jnp.mean(cen * cen, axis=-1, keepdims=True)
        y = cen * lax.rsqrt(var + EPS) * lg_ref[...] + lb_ref[...]
        yb_ref[pl.ds(c0, CONV_CHUNK), :] = (y * jax.nn.sigmoid(y)).astype(BF16)

    for n in range(TM // CONV_CHUNK):
        conv_rows(n * CONV_CHUNK)
    for n in range(3 * D // GATE_CHUNK):
        cols = slice(n * GATE_CHUNK, (n + 1) * GATE_CHUNK)
        gate_ref[:, cols] = jax.nn.sigmoid(_dot(u, wg_ref[:, cols]))
    merged = gate_ref[:, 2 * D:3 * D] * _dot(o_ref[...], woc_ref[...])

    pos = blk * TM + lax.broadcasted_iota(jnp.int32, (TM, LANES), 0) + 1
    lane = lax.broadcasted_iota(jnp.int32, (TM, LANES), 1)
    low = lane < (LANES // 2)
    means = []
    for t in range(2):
        x = xa_ref[:, t * LANES:(t + 1) * LANES]
        s2 = x + pltpu.roll(x, 1, 0)
        s4 = s2 + pltpu.roll(s2, 2, 0)
        if t == 0:
            small, big = s2, s4
        else:
            s8 = s4 + pltpu.roll(s4, 4, 0)
            small, big = s8, s8 + pltpu.roll(s8, 8, 0)
        w_small, w_big = POOL_WINDOWS[2 * t], POOL_WINDOWS[2 * t + 1]
        cnt = jnp.minimum(pos, jnp.where(low, w_small, w_big)).astype(F32)
        means.append(jnp.where(low, small[HALO:], big[HALO:]) / cnt - x[HALO:])
    diff = jnp.concatenate(means, axis=1).astype(BF16)
    ya = (_dot(diff, pw_ref[...]) + pb_ref[...]) * ps_ref[...]
    ya = _dot(ya.astype(BF16), woa_ref[...])

    merged = merged + gate_ref[:, 0:D] * ya
    merged = merged + gate_ref[:, D:2 * D] * _dot(yb_ref[...], wob_ref[...])
    out_ref[...] = h + _dot(merged.astype(BF16), wo_ref[...])


def _mixer_out(h, g, a, glu, o, wg, pw, pb, ps, cw, cb, lg, lb, woa, wob, woc, wo):
    tp = h.shape[0]
    nblk = tp // TM
    row = lambda w: pl.BlockSpec((TM, w), lambda i: (i, 0))
    halo = lambda w: pl.BlockSpec((HALO, w), lambda i: (jnp.maximum(i * (TM // HALO) - 1, 0), 0))
    const = lambda x: pl.BlockSpec(x.shape, lambda i: (0, 0))
    return pl.pallas_call(
        _mixer_out_kernel,
        grid=(nblk,),
        in_specs=[row(D), const(g), row(POOL_W), halo(POOL_W), row(CONV_W), halo(CONV_W), row(FOX_W),
                  const(wg), const(pw), const(pb), const(ps), const(cw), const(cb), const(lg), const(lb),
                  const(woa), const(wob), const(woc), const(wo)],
        out_specs=row(D),
        out_shape=jax.ShapeDtypeStruct((tp, D), F32),
        scratch_shapes=[pltpu.VMEM((TM + HALO, POOL_W), F32), pltpu.VMEM((TM + HALO, CONV_W), F32),
                        pltpu.VMEM((7, PH_ROWS, CONV_W), F32), pltpu.VMEM((TM, CONV_W), BF16),
                        pltpu.VMEM((TM, 3 * D), F32)],
        compiler_params=pltpu.CompilerParams(dimension_semantics=("arbitrary",),
                                             vmem_limit_bytes=VMEM_LIMIT),
        name="mixer_out",
    )(h, g, a, a, glu, glu, o, wg, pw, pb, ps, cw, cb, lg, lb, woa, wob, woc, wo)


TS = 1024
SEG = 256
SLOT_LANE = EPG


def _route(lt):
    gl = [lt[g:g + 1, :] for g in range(N_GROUPS)]
    gmax = functools.reduce(jnp.maximum, gl)
    gsum = functools.reduce(lambda x, y: x + y, [jnp.exp(x - gmax) for x in gl])
    gw = 1.0 / gsum
    sel, taken = [], jnp.zeros_like(gmax, dtype=jnp.bool_)
    for g in range(N_GROUPS):
        s = jnp.logical_and(gl[g] == gmax, jnp.logical_not(taken))
        sel.append(s)
        taken = jnp.logical_or(taken, s)
    el = []
    for j in range(EPG):
        e = jnp.zeros_like(gmax)
        for g in range(N_GROUPS):
            e = jnp.where(sel[g], lt[N_GROUPS + g * EPG + j:N_GROUPS + g * EPG + j + 1, :], e)
        el.append(e)
    neg = jnp.full_like(gmax, -jnp.inf)
    top1 = functools.reduce(jnp.maximum, el)
    is1, taken = [], jnp.zeros_like(gmax, dtype=jnp.bool_)
    for j in range(EPG):
        s = jnp.logical_and(el[j] == top1, jnp.logical_not(taken))
        is1.append(s)
        taken = jnp.logical_or(taken, s)
    rest = [jnp.where(is1[j], neg, el[j]) for j in range(EPG)]
    top2 = functools.reduce(jnp.maximum, rest)
    is2, taken = [], jnp.zeros_like(gmax, dtype=jnp.bool_)
    for j in range(EPG):
        s = jnp.logical_and(jnp.logical_and(rest[j] == top2, jnp.logical_not(is1[j])),
                            jnp.logical_not(taken))
        is2.append(s)
        taken = jnp.logical_or(taken, s)
    w1 = 1.0 / (1.0 + jnp.exp(top2 - top1))
    w2 = 1.0 - w1
    comb = [jnp.where(is1[j], w1, jnp.where(is2[j], w2, 0.0)) * gw for j in range(EPG)]
    return sel, comb


def _moe_sort_kernel(h_ref, g_ref, wr2_ref, rb_ref, tri_ref, xs_ref, cs_ref, info_ref, seg_ref):
    vn = _rms_norm(h_ref[...], g_ref[...])
    v_hi, v_lo = _split_bf16(vn)
    wr2 = wr2_ref[...]
    r2 = _dot(v_hi, wr2)
    logits = r2[:, :LANES] + r2[:, LANES:] + _dot(v_lo, wr2[:, :LANES]) + rb_ref[...]
    sel, comb = _route(logits.T)

    onehot = jnp.concatenate([s.astype(F32) for s in sel] + [jnp.zeros((8 - N_GROUPS, TS), F32)], axis=0)
    cnt = _dot(onehot.astype(BF16), tri_ref[...])
    totals = [jnp.max(cnt[g:g + 1, :], axis=1, keepdims=True) for g in range(N_GROUPS)]
    starts = [jnp.zeros((1, 1), F32)]
    for g in range(N_GROUPS - 1):
        starts.append(starts[g] + totals[g])
    slot = jnp.zeros((1, TS), F32)
    for g in range(N_GROUPS):
        slot = jnp.where(sel[g], starts[g] + cnt[g:g + 1, :] - 1.0, slot)

    info = jnp.concatenate(comb + [slot, jnp.zeros((LANES - EPG - 1, TS), F32)], axis=0).T
    info_ref[...] = info

    srow = lax.broadcasted_iota(jnp.int32, (TS, TS), 0)
    perm = jnp.where(srow == slot.astype(jnp.int32), 1.0, 0.0).astype(BF16)
    lane = lax.broadcasted_iota(jnp.int32, (TS, LANES), 1)
    c_hi = info.astype(BF16).astype(F32)
    c_lo = (info - c_hi).astype(BF16).astype(F32)
    pack = jnp.where(lane < EPG, c_hi, pltpu.roll(c_lo, EPG, 1)).astype(BF16)
    xs_ext = _dot(perm, jnp.concatenate([v_hi, pack], axis=1))
    xs_ref[...] = xs_ext[:, :D].astype(BF16)
    cs = xs_ext[:, D:]
    cs_ref[...] = cs + pltpu.roll(cs, LANES - EPG, 1)

    lane1 = lax.broadcasted_iota(jnp.int32, (1, LANES), 1)
    seg = jnp.zeros((1, LANES), F32)
    for g in range(N_GROUPS):
        seg = jnp.where(lane1 == g, starts[g], seg)
        seg = jnp.where(lane1 == N_GROUPS + g, starts[g] + totals[g], seg)
    seg_ref[0] = seg.astype(jnp.int32)


def _moe_sort(h, g, wr2, rb, tri):
    tp = h.shape[0]
    nblk = tp // TS
    const = lambda x: pl.BlockSpec(x.shape, lambda i: (0, 0))
    row = lambda w: pl.BlockSpec((TS, w), lambda i: (i, 0))
    return pl.pallas_call(
        _moe_sort_kernel,
        grid=(nblk,),
        in_specs=[row(D), const(g), const(wr2), const(rb), const(tri)],
        out_specs=[row(D), row(LANES), row(LANES), pl.BlockSpec((1, 1, LANES), lambda i: (i, 0, 0))],
        out_shape=[jax.ShapeDtypeStruct((tp, D), BF16), jax.ShapeDtypeStruct((tp, LANES), F32),
                   jax.ShapeDtypeStruct((tp, LANES), F32), jax.ShapeDtypeStruct((nblk, 1, LANES), jnp.int32)],
        compiler_params=pltpu.CompilerParams(dimension_semantics=("arbitrary",),
                                             vmem_limit_bytes=VMEM_LIMIT),
        name="moe_sort",
    )(h, g, wr2, rb, tri)


def _moe_ffn_kernel(seg_ref, xs_ref, cs_ref, info_ref, h_ref, w1_ref, w3_ref, w2_ref, out_ref, ys_ref):
    i = pl.program_id(0)
    ys_ref[...] = jnp.zeros_like(ys_ref)

    def pair(p, carry):
        c = p // N_GROUPS
        g = p % N_GROUPS
        start = seg_ref[i * LANES + g]
        end = seg_ref[i * LANES + N_GROUPS + g]
        lo = c * SEG
        hit = jnp.logical_and(jnp.logical_and(start < lo + SEG, end > lo), end > start)

        @pl.when(hit)
        def _():
            rows = pl.ds(pl.multiple_of(lo, SEG), SEG)
            x = xs_ref[rows, :]
            h1 = _dot(x, w1_ref[g])
            h3 = _dot(x, w3_ref[g])
            he = h1 * jax.nn.sigmoid(h1) * h3
            r = lo + lax.broadcasted_iota(jnp.int32, (SEG, 1), 0)
            inseg = jnp.logical_and(r >= start, r < end)
            cs = cs_ref[rows, :]
            parts = []
            for j in range(EPG):
                wcol = jnp.where(inseg, cs[:, j:j + 1], 0.0)
                parts.append((he[:, j * EXPERT_HIDDEN:(j + 1) * EXPERT_HIDDEN] * wcol).astype(BF16))
            ys_ref[rows, :] += _dot(jnp.concatenate(parts, axis=1), w2_ref[g])

        return carry

    lax.fori_loop(0, (TS // SEG) * N_GROUPS, pair, 0)

    slot = info_ref[:, SLOT_LANE:SLOT_LANE + 1].astype(jnp.int32)
    scol = lax.broadcasted_iota(jnp.int32, (TS, TS), 1)
    unperm = jnp.where(scol == slot, 1.0, 0.0).astype(BF16)
    out_ref[...] = h_ref[...] + _dot(unperm, ys_ref[...].astype(BF16))


def _moe_ffn(seg, xs, cs, info, h, w1g, w3g, w2g):
    tp = h.shape[0]
    nblk = tp // TS
    row = lambda w: pl.BlockSpec((TS, w), lambda i, s: (i, 0))
    resident = lambda x: pl.BlockSpec(x.shape, lambda i, s: (0, 0, 0), pipeline_mode=pl.Buffered(1))
    return pl.pallas_call(
        _moe_ffn_kernel,
        grid_spec=pltpu.PrefetchScalarGridSpec(
            num_scalar_prefetch=1,
            grid=(nblk,),
            in_specs=[row(D), row(LANES), row(LANES), row(D), resident(w1g), resident(w3g), resident(w2g)],
            out_specs=row(D),
            scratch_shapes=[pltpu.VMEM((TS, D), F32)]),
        out_shape=jax.ShapeDtypeStruct((tp, D), F32),
        compiler_params=pltpu.CompilerParams(dimension_semantics=("arbitrary",),
                                             vmem_limit_bytes=MOE_VMEM_LIMIT),
        name="moe_ffn",
    )(seg, xs, cs, info, h, w1g, w3g, w2g)


def _final_norm_kernel(h_ref, g_ref, out_ref):
    out_ref[...] = _rms_norm(h_ref[...], g_ref[...])


def _final_norm(h, g, batch, seq_out):
    nblk = seq_out // TM
    return pl.pallas_call(
        _final_norm_kernel,
        grid=(batch, nblk),
        in_specs=[pl.BlockSpec((pl.Element(TM), pl.Element(D)),
                               lambda b, r: (pl.multiple_of(b * LP + N_META + r * TM, N_META), 0)),
                  pl.BlockSpec((1, D), lambda b, r: (0, 0))],
        out_specs=pl.BlockSpec((TM, D), lambda b, r: (b * nblk + r, 0)),
        out_shape=jax.ShapeDtypeStruct((batch * seq_out, D), F32),
        compiler_params=pltpu.CompilerParams(dimension_semantics=("arbitrary", "arbitrary")),
        name="final_norm",
    )(h, g)


def _pad_lanes(x, n=LANES):
    return jnp.pad(x, ((0, 0), (0, n - x.shape[1])))


def _pad_rows(w, n):
    return jnp.pad(w, ((0, n - w.shape[0]), (0, 0)))


def _hi_lo_weight(w):
    hi = w.astype(BF16)
    lo = (w - hi.astype(F32)).astype(BF16)
    return jnp.concatenate([_pad_lanes(hi), _pad_lanes(lo)], axis=1)


def _per_head_cols(w):
    k = w.shape[0]
    w = w.reshape(k, HEADS, HEAD_DIM)
    return jnp.pad(w, ((0, 0), (0, 0), (0, HEAD_PAD - HEAD_DIM))).reshape(k, ATT_W)


def _bias_select():
    sel = jnp.zeros((LANES, 2, HEADS, HEAD_PAD), F32)
    one_lane = N_FPIECE * HEADS
    for p in range(N_FPIECE):
        for hd in range(HEADS):
            sel = sel.at[p * HEADS + hd, 0, hd, HEAD_DIM + p].set(1.0)
            sel = sel.at[p * HEADS + hd, 1, hd, HEAD_DIM + N_FPIECE + p].set(-1.0)
    sel = sel.at[one_lane, 0, :, HEAD_DIM + N_FPIECE:HEAD_DIM + 2 * N_FPIECE].set(1.0)
    sel = sel.at[one_lane, 1, :, HEAD_DIM:HEAD_DIM + N_FPIECE].set(1.0)
    return sel.reshape(LANES, 2 * ATT_W).astype(BF16)


def kernel(x, meta, norm1_g, w_in, b_forget, pool_w, pool_b, pool_scale, conv_w, conv_b, conv_ln_g,
           conv_ln_b, w_out_a, w_out_b, w_out_c, w_o, norm2_g, router_g, router_g_b, router_e,
           router_e_b, exp_w1, exp_w3, exp_w2, final_g):
    B, L, _ = x.shape
    depth = w_in.shape[0]
    seq = N_META + L
    assert seq <= LP and LP % TM == 0 and LP % TQ == 0 and (B * LP) % TS == 0
    m = jnp.broadcast_to(meta.astype(x.dtype)[None], (B, N_META, D))
    h = jnp.concatenate([m, x, jnp.zeros((B, LP - seq, D), x.dtype)], axis=1).reshape(B * LP, D)
    tri = jnp.tril(jnp.ones((TM, TM), BF16))
    tri_u = jnp.triu(jnp.ones((TS, TS), BF16))
    sel = _bias_select()

    c_abc = POOL_W + 2 * CONV_W
    c_q = c_abc + FOX_W
    c_k = c_q + FOX_W
    c_v = c_k + FOX_W
    c_f = c_v + HEADS
    for l in range(depth):
        wl = w_in[l]
        wabc = wl[:, :c_abc].astype(BF16)
        wq = _per_head_cols(wl[:, c_abc:c_q] * (HEAD_DIM ** -0.5 * LOG2E))
        wk = _per_head_cols(wl[:, c_q:c_k])
        wqk = jnp.concatenate([wq, wk], axis=1).astype(BF16)
        wvt = _per_head_cols(wl[:, c_k:c_v]).T.astype(BF16)
        wf = _pad_lanes(wl[:, c_v:c_f]).astype(BF16)
        wg = wl[:, c_f:].astype(BF16)
        bf = _pad_lanes(b_forget[l][None, :])
        a, glu, q, k, vt = _in_proj(h, norm1_g[l][None, :], wabc, wqk, wvt, wf, bf, tri, sel, B)
        o = _attention(q, k, vt, B)

        pw = jax.scipy.linalg.block_diag(*[pool_w[l, g] for g in range(pool_w.shape[1])]).astype(BF16)
        h = _mixer_out(h, norm1_g[l][None, :], a, glu, o, wg, pw, pool_b[l].reshape(1, POOL_W),
                       pool_scale[l][None, :], _pad_rows(conv_w[l], 32), conv_b[l][None, :],
                       conv_ln_g[l][None, :], conv_ln_b[l][None, :],
                       w_out_a[l].astype(BF16), w_out_b[l].astype(BF16), w_out_c[l].astype(BF16),
                       w_o[l].astype(BF16))

        wr2 = _hi_lo_weight(jnp.concatenate([router_g[l], router_e[l]], axis=1))
        rb = _pad_lanes(jnp.concatenate([router_g_b[l], router_e_b[l]])[None, :])
        w1g = exp_w1[l].reshape(N_GROUPS, EPG, D, EXPERT_HIDDEN).transpose(0, 2, 1, 3).reshape(N_GROUPS, D, GROUP_HIDDEN)
        w3g = exp_w3[l].reshape(N_GROUPS, EPG, D, EXPERT_HIDDEN).transpose(0, 2, 1, 3).reshape(N_GROUPS, D, GROUP_HIDDEN)
        w2g = exp_w2[l].reshape(N_GROUPS, GROUP_HIDDEN, D)
        xs, cs, info, seg = _moe_sort(h, norm2_g[l][None, :], wr2, rb, tri_u)
        h = _moe_ffn(seg.reshape(-1), xs, cs, info, h, w1g.astype(BF16), w3g.astype(BF16), w2g.astype(BF16))

    assert L % TM == 0
    return _final_norm(h, final_g[None, :], B, L).reshape(B, L, D)
```

```python
import functools
import math

import jax
import jax.numpy as jnp
from jax import lax
from jax.experimental import pallas as pl
from jax.experimental.pallas import tpu as pltpu

D = 1024
N_META = 16
EPS = 1e-6
POOL_W = 256
POOL_WINDOWS = (2, 4, 8, 16)
CONV_W = 256
CONV_K = 31
HEADS = 8
HEAD_DIM = 64
FOX_W = HEADS * HEAD_DIM
N_GROUPS = 4
EPG = 4
N_EXPERTS = 16
EXPERT_HIDDEN = 256
GROUP_HIDDEN = EPG * EXPERT_HIDDEN

LANES = 128
HEAD_PAD = LANES
ATT_W = HEADS * HEAD_PAD
LP = 8704
TM = 512
BLOCKS_PER_SEQ = LP // TM
HALO = 32
TQ = 512
N_FPIECE = 3
VMEM_LIMIT = 56 * 1024 * 1024
MOE_VMEM_LIMIT = 60 * 1024 * 1024
LOG2E = math.log2(math.e)

F32 = jnp.float32
BF16 = jnp.bfloat16


def _dot(a, b):
    return jnp.dot(a, b, preferred_element_type=F32)


def _dot_nt(a, b):
    return lax.dot_general(a, b, (((1,), (1,)), ((), ())), preferred_element_type=F32)


def _rms_norm(x, g):
    ms = jnp.mean(x * x, axis=-1, keepdims=True)
    return x * lax.rsqrt(ms + EPS) * g


def _split_bf16(x):
    hi = x.astype(BF16)
    lo = (x - hi.astype(F32)).astype(BF16)
    return hi, lo


CONV_CHUNK = 64
PH_ROWS = TM + HALO - 8


def _in_proj_kernel(h_ref, g_ref, wabc_ref, wqk_ref, wvt_ref, wf_ref, bf_ref, tri_ref, sel_ref,
                    pw_ref, pb_ref, ps_ref, cw_ref, cb_ref, lg_ref, lb_ref,
                    ya_ref, yb_ref, q_ref, k_ref, vt_ref, carry_ref, xa_ref, xg_ref, ph_ref):
    i = pl.program_id(0)
    blk = i % BLOCKS_PER_SEQ

    @pl.when(blk == 0)
    def _():
        carry_ref[...] = jnp.zeros_like(carry_ref)
        xa_ref[0:HALO, :] = jnp.zeros((HALO, POOL_W), F32)
        xg_ref[0:HALO, :] = jnp.zeros((HALO, CONV_W), F32)

    u = _rms_norm(h_ref[...], g_ref[...]).astype(BF16)

    abc = _dot(u, wabc_ref[...])
    xa_ref[HALO:, :] = abc[:, :POOL_W]
    xg_ref[HALO:, :] = abc[:, POOL_W:POOL_W + CONV_W] * jax.nn.sigmoid(abc[:, POOL_W + CONV_W:])

    pos = blk * TM + lax.broadcasted_iota(jnp.int32, (TM, LANES), 0) + 1
    lane = lax.broadcasted_iota(jnp.int32, (TM, LANES), 1)
    low = lane < (LANES // 2)
    means = []
    for t in range(2):
        x = xa_ref[:, t * LANES:(t + 1) * LANES]
        s2 = x + pltpu.roll(x, 1, 0)
        s4 = s2 + pltpu.roll(s2, 2, 0)
        if t == 0:
            small, big = s2, s4
        else:
            s8 = s4 + pltpu.roll(s4, 4, 0)
            small, big = s8, s8 + pltpu.roll(s8, 8, 0)
        w_small, w_big = POOL_WINDOWS[2 * t], POOL_WINDOWS[2 * t + 1]
        cnt = jnp.minimum(pos, jnp.where(low, w_small, w_big)).astype(F32)
        means.append(jnp.where(low, small[HALO:], big[HALO:]) / cnt - x[HALO:])
    diff = jnp.concatenate(means, axis=1).astype(BF16)
    ya_ref[...] = ((_dot(diff, pw_ref[...]) + pb_ref[...]) * ps_ref[...]).astype(BF16)

    for r in range(1, 8):
        ph_ref[r - 1] = xg_ref[r:r + PH_ROWS, :]
    cw = cw_ref[...]
    base = HALO - (CONV_K - 1)

    def conv_rows(c0):
        acc = jnp.zeros((CONV_CHUNK, CONV_W), F32) + cb_ref[...]
        for j in range(CONV_K):
            off = base + j
            r = off % 8
            rows = pl.ds(c0 + (off - r), CONV_CHUNK)
            tap = xg_ref[rows, :] if r == 0 else ph_ref[r - 1, rows, :]
            acc = acc + tap * cw[j:j + 1, :]
        mu = jnp.mean(acc, axis=-1, keepdims=True)
        cen = acc - mu
        var = jnp.mean(cen * cen, axis=-1, keepdims=True)
        y = cen * lax.rsqrt(var + EPS) * lg_ref[...] + lb_ref[...]
        yb_ref[c0:c0 + CONV_CHUNK, :] = (y * jax.nn.sigmoid(y)).astype(BF16)

    fg = _dot(u, wf_ref[...]) + bf_ref[...]
    logf = jnp.minimum(fg, 0.0) - jnp.log1p(jnp.exp(-jnp.abs(fg)))
    l_hi, l_lo = _split_bf16(logf)
    cum2 = _dot(tri_ref[...], jnp.concatenate([l_hi, l_lo], axis=1))
    cum = cum2[:, :LANES] + cum2[:, LANES:] + carry_ref[...]
    carry_ref[...] = cum[TM - 1:TM, :]

    lane = lax.broadcasted_iota(jnp.int32, (TM, LANES), 1)
    fv = jnp.where(lane < HEADS, cum * LOG2E, 0.0)
    f_hi = fv.astype(BF16).astype(F32)
    r1 = fv - f_hi
    f_mid = r1.astype(BF16).astype(F32)
    f_lo = (r1 - f_mid).astype(BF16).astype(F32)
    fcat = f_hi + pltpu.roll(f_mid, HEADS, 1) + pltpu.roll(f_lo, 2 * HEADS, 1)
    fcat = jnp.where(lane == N_FPIECE * HEADS, 1.0, fcat).astype(BF16)

    n_chunks = TM // CONV_CHUNK
    qk_chunk = 2 * ATT_W // n_chunks
    for n in range(n_chunks):
        cols = slice(n * qk_chunk, (n + 1) * qk_chunk)
        qk = (_dot(u, wqk_ref[:, cols]) + _dot(fcat, sel_ref[:, cols])).astype(BF16)
        if n < n_chunks // 2:
            q_ref[:, cols] = qk
        else:
            k_ref[:, n * qk_chunk - ATT_W:(n + 1) * qk_chunk - ATT_W] = qk
        conv_rows(n * CONV_CHUNK)

    xa_ref[0:HALO, :] = xa_ref[TM:TM + HALO, :]
    xg_ref[0:HALO, :] = xg_ref[TM:TM + HALO, :]

    vt = _dot_nt(wvt_ref[...], u)
    row = lax.broadcasted_iota(jnp.int32, (ATT_W, TM), 0)
    vt_ref[0] = jnp.where(row % HEAD_PAD == HEAD_DIM, 1.0, vt).astype(BF16)


def _in_proj(h, g, wabc, wqk, wvt, wf, bf, tri, sel, pool_conv, batch):
    tp = h.shape[0]
    nblk = tp // TM
    row = lambda w: pl.BlockSpec((TM, w), lambda i: (i, 0))
    const = lambda x: pl.BlockSpec(x.shape, lambda i: (0, 0))
    return pl.pallas_call(
        _in_proj_kernel,
        grid=(nblk,),
        in_specs=[row(D), const(g), const(wabc), const(wqk), const(wvt), const(wf), const(bf),
                  const(tri), const(sel)] + [const(p) for p in pool_conv],
        out_specs=[row(POOL_W), row(CONV_W), row(ATT_W), row(ATT_W),
                   pl.BlockSpec((1, ATT_W, TM), lambda i: (i // BLOCKS_PER_SEQ, 0, i % BLOCKS_PER_SEQ))],
        out_shape=[jax.ShapeDtypeStruct((tp, POOL_W), BF16), jax.ShapeDtypeStruct((tp, CONV_W), BF16),
                   jax.ShapeDtypeStruct((tp, ATT_W), BF16), jax.ShapeDtypeStruct((tp, ATT_W), BF16),
                   jax.ShapeDtypeStruct((batch, ATT_W, LP), BF16)],
        scratch_shapes=[pltpu.VMEM((1, LANES), F32),
                        pltpu.VMEM((TM + HALO, POOL_W), F32), pltpu.VMEM((TM + HALO, CONV_W), F32),
                        pltpu.VMEM((7, PH_ROWS, CONV_W), F32)],
        compiler_params=pltpu.CompilerParams(dimension_semantics=("arbitrary",),
                                             vmem_limit_bytes=VMEM_LIMIT),
        name="in_proj",
    )(h, g, wabc, wqk, wvt, wf, bf, tri, sel, *pool_conv)


HEADS_PER_STEP = 4


def _attn_kernel(q_ref, k_ref, vt_ref, o_ref, m_ref, acc_ref, s_ref, mt_ref):
    qi = pl.program_id(2)
    m_ref[...] = jnp.full_like(m_ref, -jnp.inf)
    acc_ref[...] = jnp.zeros_like(acc_ref)

    def scores(j, slot, masked=False):
        start = pl.multiple_of(j * TQ, TQ)
        for hd in range(HEADS_PER_STEP):
            cols = slice(hd * HEAD_PAD, (hd + 1) * HEAD_PAD)
            s = _dot_nt(k_ref[pl.ds(start, TQ), cols], q_ref[:, cols])
            if masked:
                key = lax.broadcasted_iota(jnp.int32, (TQ, TQ), 0)
                qry = lax.broadcasted_iota(jnp.int32, (TQ, TQ), 1)
                s = jnp.where(key <= qry, s, -jnp.inf)
            s_ref[slot, hd] = s
            mt_ref[slot, hd] = jnp.max(s, axis=0, keepdims=True)

    def values(j, slot):
        start = pl.multiple_of(j * TQ, TQ)
        for hd in range(HEADS_PER_STEP):
            cols = slice(hd * HEAD_PAD, (hd + 1) * HEAD_PAD)
            m_old = m_ref[hd]
            m_new = jnp.maximum(m_old, mt_ref[slot, hd])
            p = jnp.exp2(s_ref[slot, hd] - m_new).astype(BF16)
            alpha = jnp.exp2(m_old - m_new)
            pv = _dot(vt_ref[0, cols, pl.ds(start, TQ)], p)
            acc_ref[hd] = alpha * acc_ref[hd] + pv
            m_ref[hd] = m_new

    @pl.when(qi == 0)
    def _():
        scores(0, 0, masked=True)
        values(0, 0)

    @pl.when(qi > 0)
    def _():
        scores(0, 0)

        def body(t, c):
            j = 2 * t
            scores(j + 1, 1)
            values(j, 0)
            scores(j + 2, 0)
            values(j + 1, 1)
            return c

        last = qi - 1
        lax.fori_loop(0, last // 2, body, 0)

        @pl.when(last % 2 == 0)
        def _():
            scores(qi, 1, masked=True)
            values(last, 0)
            values(qi, 1)

        @pl.when(last % 2 == 1)
        def _():
            scores(last, 1)
            values(last - 1, 0)
            scores(qi, 0, masked=True)
            values(last, 1)
            values(qi, 0)

    outs = []
    for hd in range(HEADS_PER_STEP):
        acc = acc_ref[hd]
        outs.append(acc[:HEAD_DIM, :] / acc[HEAD_DIM:HEAD_DIM + 1, :])
    o_ref[...] = jnp.concatenate(outs, axis=0).T.astype(BF16)


def _attention(q, k, vt, batch):
    nq = LP // TQ
    w = HEADS_PER_STEP * HEAD_PAD
    return pl.pallas_call(
        _attn_kernel,
        grid=(batch, HEADS // HEADS_PER_STEP, nq),
        in_specs=[pl.BlockSpec((TQ, w), lambda b, h, i: (b * nq + i, h)),
                  pl.BlockSpec((LP, w), lambda b, h, i: (b, h)),
                  pl.BlockSpec((1, w, LP), lambda b, h, i: (b, h, 0))],
        out_specs=pl.BlockSpec((TQ, HEADS_PER_STEP * HEAD_DIM), lambda b, h, i: (b * nq + i, h)),
        out_shape=jax.ShapeDtypeStruct((batch * LP, FOX_W), BF16),
        scratch_shapes=[pltpu.VMEM((HEADS_PER_STEP, 1, TQ), F32),
                        pltpu.VMEM((HEADS_PER_STEP, HEAD_PAD, TQ), F32),
                        pltpu.VMEM((2, HEADS_PER_STEP, TQ, TQ), F32),
                        pltpu.VMEM((2, HEADS_PER_STEP, 1, TQ), F32)],
        compiler_params=pltpu.CompilerParams(dimension_semantics=("arbitrary", "arbitrary", "arbitrary"),
                                             vmem_limit_bytes=VMEM_LIMIT),
        name="fox_attention",
    )(q, k, vt)


def _mixer_out_kernel(h_ref, g_ref, ya_ref, yb_ref, o_ref, wg_ref,
                      woa_ref, wob_ref, woc_ref, wo_ref, out_ref):
    h = h_ref[...]
    u = _rms_norm(h, g_ref[...]).astype(BF16)
    branches = (_dot(ya_ref[...], woa_ref[...]), _dot(yb_ref[...], wob_ref[...]),
                _dot(o_ref[...], woc_ref[...]))
    merged = None
    for n, y in enumerate(branches):
        gated = jax.nn.sigmoid(_dot(u, wg_ref[:, n * D:(n + 1) * D])) * y
        merged = gated if merged is None else merged + gated
    out_ref[...] = h + _dot(merged.astype(BF16), wo_ref[...])


def _mixer_out(h, g, ya, yb, o, wg, woa, wob, woc, wo):
    tp = h.shape[0]
    nblk = tp // TM
    row = lambda w: pl.BlockSpec((TM, w), lambda i: (i, 0))
    const = lambda x: pl.BlockSpec(x.shape, lambda i: (0, 0))
    return pl.pallas_call(
        _mixer_out_kernel,
        grid=(nblk,),
        in_specs=[row(D), const(g), row(POOL_W), row(CONV_W), row(FOX_W),
                  const(wg), const(woa), const(wob), const(woc), const(wo)],
        out_specs=row(D),
        out_shape=jax.ShapeDtypeStruct((tp, D), F32),
        compiler_params=pltpu.CompilerParams(dimension_semantics=("arbitrary",),
                                             vmem_limit_bytes=VMEM_LIMIT),
        name="mixer_out",
    )(h, g, ya, yb, o, wg, woa, wob, woc, wo)


TS = 1024
SEG = 256
SLOT_LANE = EPG


def _route(lt):
    gl = [lt[g:g + 1, :] for g in range(N_GROUPS)]
    gmax = functools.reduce(jnp.maximum, gl)
    gsum = functools.reduce(lambda x, y: x + y, [jnp.exp(x - gmax) for x in gl])
    gw = 1.0 / gsum
    sel, taken = [], jnp.zeros_like(gmax, dtype=jnp.bool_)
    for g in range(N_GROUPS):
        s = jnp.logical_and(gl[g] == gmax, jnp.logical_not(taken))
        sel.append(s)
        taken = jnp.logical_or(taken, s)
    el = []
    for j in range(EPG):
        e = jnp.zeros_like(gmax)
        for g in range(N_GROUPS):
            e = jnp.where(sel[g], lt[N_GROUPS + g * EPG + j:N_GROUPS + g * EPG + j + 1, :], e)
        el.append(e)
    neg = jnp.full_like(gmax, -jnp.inf)
    top1 = functools.reduce(jnp.maximum, el)
    is1, taken = [], jnp.zeros_like(gmax, dtype=jnp.bool_)
    for j in range(EPG):
        s = jnp.logical_and(el[j] == top1, jnp.logical_not(taken))
        is1.append(s)
        taken = jnp.logical_or(taken, s)
    rest = [jnp.where(is1[j], neg, el[j]) for j in range(EPG)]
    top2 = functools.reduce(jnp.maximum, rest)
    is2, taken = [], jnp.zeros_like(gmax, dtype=jnp.bool_)
    for j in range(EPG):
        s = jnp.logical_and(jnp.logical_and(rest[j] == top2, jnp.logical_not(is1[j])),
                            jnp.logical_not(taken))
        is2.append(s)
        taken = jnp.logical_or(taken, s)
    w1 = 1.0 / (1.0 + jnp.exp(top2 - top1))
    w2 = 1.0 - w1
    comb = [jnp.where(is1[j], w1, jnp.where(is2[j], w2, 0.0)) * gw for j in range(EPG)]
    return sel, comb


def _moe_sort_kernel(h_ref, g_ref, wr2_ref, rb_ref, tri_ref, xs_ref, cs_ref, info_ref, seg_ref):
    vn = _rms_norm(h_ref[...], g_ref[...])
    v_hi, v_lo = _split_bf16(vn)
    wr2 = wr2_ref[...]
    r2 = _dot(v_hi, wr2)
    logits = r2[:, :LANES] + r2[:, LANES:] + _dot(v_lo, wr2[:, :LANES]) + rb_ref[...]
    sel, comb = _route(logits.T)

    onehot = jnp.concatenate([s.astype(F32) for s in sel] + [jnp.zeros((8 - N_GROUPS, TS), F32)], axis=0)
    cnt = _dot(onehot.astype(BF16), tri_ref[...])
    totals = [jnp.max(cnt[g:g + 1, :], axis=1, keepdims=True) for g in range(N_GROUPS)]
    starts = [jnp.zeros((1, 1), F32)]
    for g in range(N_GROUPS - 1):
        starts.append(starts[g] + totals[g])
    slot = jnp.zeros((1, TS), F32)
    for g in range(N_GROUPS):
        slot = jnp.where(sel[g], starts[g] + cnt[g:g + 1, :] - 1.0, slot)

    info = jnp.concatenate(comb + [slot, jnp.zeros((LANES - EPG - 1, TS), F32)], axis=0).T
    info_ref[...] = info

    srow = lax.broadcasted_iota(jnp.int32, (TS, TS), 0)
    perm = jnp.where(srow == slot.astype(jnp.int32), 1.0, 0.0).astype(BF16)
    lane = lax.broadcasted_iota(jnp.int32, (TS, LANES), 1)
    c_hi = info.astype(BF16).astype(F32)
    c_lo = (info - c_hi).astype(BF16).astype(F32)
    pack = jnp.where(lane < EPG, c_hi, pltpu.roll(c_lo, EPG, 1)).astype(BF16)
    xs_ext = _dot(perm, jnp.concatenate([v_hi, pack], axis=1))
    xs_ref[...] = xs_ext[:, :D].astype(BF16)
    cs = xs_ext[:, D:]
    cs_ref[...] = cs + pltpu.roll(cs, LANES - EPG, 1)

    lane1 = lax.broadcasted_iota(jnp.int32, (1, LANES), 1)
    seg = jnp.zeros((1, LANES), F32)
    for g in range(N_GROUPS):
        seg = jnp.where(lane1 == g, starts[g], seg)
        seg = jnp.where(lane1 == N_GROUPS + g, starts[g] + totals[g], seg)
    seg_ref[0] = seg.astype(jnp.int32)


def _moe_sort(h, g, wr2, rb, tri):
    tp = h.shape[0]
    nblk = tp // TS
    const = lambda x: pl.BlockSpec(x.shape, lambda i: (0, 0))
    row = lambda w: pl.BlockSpec((TS, w), lambda i: (i, 0))
    return pl.pallas_call(
        _moe_sort_kernel,
        grid=(nblk,),
        in_specs=[row(D), const(g), const(wr2), const(rb), const(tri)],
        out_specs=[row(D), row(LANES), row(LANES), pl.BlockSpec((1, 1, LANES), lambda i: (i, 0, 0))],
        out_shape=[jax.ShapeDtypeStruct((tp, D), BF16), jax.ShapeDtypeStruct((tp, LANES), F32),
                   jax.ShapeDtypeStruct((tp, LANES), F32), jax.ShapeDtypeStruct((nblk, 1, LANES), jnp.int32)],
        compiler_params=pltpu.CompilerParams(dimension_semantics=("arbitrary",),
                                             vmem_limit_bytes=VMEM_LIMIT),
        name="moe_sort",
    )(h, g, wr2, rb, tri)


def _moe_ffn_kernel(seg_ref, xs_ref, cs_ref, info_ref, h_ref, w1_ref, w3_ref, w2_ref, out_ref, ys_ref):
    i = pl.program_id(0)
    ys_ref[...] = jnp.zeros_like(ys_ref)

    def pair(p, carry):
        c = p // N_GROUPS
        g = p % N_GROUPS
        start = seg_ref[i * LANES + g]
        end = seg_ref[i * LANES + N_GROUPS + g]
        lo = c * SEG
        hit = jnp.logical_and(jnp.logical_and(start < lo + SEG, end > lo), end > start)

        @pl.when(hit)
        def _():
            rows = pl.ds(pl.multiple_of(lo, SEG), SEG)
            x = xs_ref[rows, :]
            h1 = _dot(x, w1_ref[g])
            h3 = _dot(x, w3_ref[g])
            he = h1 * jax.nn.sigmoid(h1) * h3
            r = lo + lax.broadcasted_iota(jnp.int32, (SEG, 1), 0)
            inseg = jnp.logical_and(r >= start, r < end)
            cs = cs_ref[rows, :]
            parts = []
            for j in range(EPG):
                wcol = jnp.where(inseg, cs[:, j:j + 1], 0.0)
                parts.append((he[:, j * EXPERT_HIDDEN:(j + 1) * EXPERT_HIDDEN] * wcol).astype(BF16))
            ys_ref[rows, :] += _dot(jnp.concatenate(parts, axis=1), w2_ref[g])

        return carry

    lax.fori_loop(0, (TS // SEG) * N_GROUPS, pair, 0)

    slot = info_ref[:, SLOT_LANE:SLOT_LANE + 1].astype(jnp.int32)
    scol = lax.broadcasted_iota(jnp.int32, (TS, TS), 1)
    unperm = jnp.where(scol == slot, 1.0, 0.0).astype(BF16)
    out_ref[...] = h_ref[...] + _dot(unperm, ys_ref[...].astype(BF16))


def _moe_ffn(seg, xs, cs, info, h, w1g, w3g, w2g):
    tp = h.shape[0]
    nblk = tp // TS
    row = lambda w: pl.BlockSpec((TS, w), lambda i, s: (i, 0))
    resident = lambda x: pl.BlockSpec(x.shape, lambda i, s: (0, 0, 0), pipeline_mode=pl.Buffered(1))
    return pl.pallas_call(
        _moe_ffn_kernel,
        grid_spec=pltpu.PrefetchScalarGridSpec(
            num_scalar_prefetch=1,
            grid=(nblk,),
            in_specs=[row(D), row(LANES), row(LANES), row(D), resident(w1g), resident(w3g), resident(w2g)],
            out_specs=row(D),
            scratch_shapes=[pltpu.VMEM((TS, D), F32)]),
        out_shape=jax.ShapeDtypeStruct((tp, D), F32),
        compiler_params=pltpu.CompilerParams(dimension_semantics=("arbitrary",),
                                             vmem_limit_bytes=MOE_VMEM_LIMIT),
        name="moe_ffn",
    )(seg, xs, cs, info, h, w1g, w3g, w2g)


def _final_norm_kernel(h_ref, g_ref, out_ref):
    out_ref[...] = _rms_norm(h_ref[...], g_ref[...])


def _final_norm(h, g, batch, seq_out):
    nblk = seq_out // TM
    return pl.pallas_call(
        _final_norm_kernel,
        grid=(batch, nblk),
        in_specs=[pl.BlockSpec((pl.Element(TM), pl.Element(D)),
                               lambda b, r: (pl.multiple_of(b * LP + N_META + r * TM, N_META), 0)),
                  pl.BlockSpec((1, D), lambda b, r: (0, 0))],
        out_specs=pl.BlockSpec((TM, D), lambda b, r: (b * nblk + r, 0)),
        out_shape=jax.ShapeDtypeStruct((batch * seq_out, D), F32),
        compiler_params=pltpu.CompilerParams(dimension_semantics=("arbitrary", "arbitrary")),
        name="final_norm",
    )(h, g)


def _pad_lanes(x, n=LANES):
    return jnp.pad(x, ((0, 0), (0, n - x.shape[1])))


def _pad_rows(w, n):
    return jnp.pad(w, ((0, n - w.shape[0]), (0, 0)))


def _hi_lo_weight(w):
    hi = w.astype(BF16)
    lo = (w - hi.astype(F32)).astype(BF16)
    return jnp.concatenate([_pad_lanes(hi), _pad_lanes(lo)], axis=1)


def _per_head_cols(w):
    k = w.shape[0]
    w = w.reshape(k, HEADS, HEAD_DIM)
    return jnp.pad(w, ((0, 0), (0, 0), (0, HEAD_PAD - HEAD_DIM))).reshape(k, ATT_W)


def _bias_select():
    sel = jnp.zeros((LANES, 2, HEADS, HEAD_PAD), F32)
    one_lane = N_FPIECE * HEADS
    for p in range(N_FPIECE):
        for hd in range(HEADS):
            sel = sel.at[p * HEADS + hd, 0, hd, HEAD_DIM + p].set(1.0)
            sel = sel.at[p * HEADS + hd, 1, hd, HEAD_DIM + N_FPIECE + p].set(-1.0)
    sel = sel.at[one_lane, 0, :, HEAD_DIM + N_FPIECE:HEAD_DIM + 2 * N_FPIECE].set(1.0)
    sel = sel.at[one_lane, 1, :, HEAD_DIM:HEAD_DIM + N_FPIECE].set(1.0)
    return sel.reshape(LANES, 2 * ATT_W).astype(BF16)


def kernel(x, meta, norm1_g, w_in, b_forget, pool_w, pool_b, pool_scale, conv_w, conv_b, conv_ln_g,
           conv_ln_b, w_out_a, w_out_b, w_out_c, w_o, norm2_g, router_g, router_g_b, router_e,
           router_e_b, exp_w1, exp_w3, exp_w2, final_g):
    B, L, _ = x.shape
    depth = w_in.shape[0]
    seq = N_META + L
    assert seq <= LP and LP % TM == 0 and LP % TQ == 0 and (B * LP) % TS == 0
    m = jnp.broadcast_to(meta.astype(x.dtype)[None], (B, N_META, D))
    h = jnp.concatenate([m, x, jnp.zeros((B, LP - seq, D), x.dtype)], axis=1).reshape(B * LP, D)
    tri = jnp.tril(jnp.ones((TM, TM), BF16))
    tri_u = jnp.triu(jnp.ones((TS, TS), BF16))
    sel = _bias_select()

    c_abc = POOL_W + 2 * CONV_W
    c_q = c_abc + FOX_W
    c_k = c_q + FOX_W
    c_v = c_k + FOX_W
    c_f = c_v + HEADS
    for l in range(depth):
        wl = w_in[l]
        wabc = wl[:, :c_abc].astype(BF16)
        wq = _per_head_cols(wl[:, c_abc:c_q] * (HEAD_DIM ** -0.5 * LOG2E))
        wk = _per_head_cols(wl[:, c_q:c_k])
        wqk = jnp.concatenate([wq, wk], axis=1).astype(BF16)
        wvt = _per_head_cols(wl[:, c_k:c_v]).T.astype(BF16)
        wf = _pad_lanes(wl[:, c_v:c_f]).astype(BF16)
        wg = wl[:, c_f:].astype(BF16)
        bf = _pad_lanes(b_forget[l][None, :])
        pw = jax.scipy.linalg.block_diag(*[pool_w[l, g] for g in range(pool_w.shape[1])]).astype(BF16)
        pool_conv = (pw, pool_b[l].reshape(1, POOL_W), pool_scale[l][None, :], _pad_rows(conv_w[l], 32),
                     conv_b[l][None, :], conv_ln_g[l][None, :], conv_ln_b[l][None, :])
        ya, yb, q, k, vt = _in_proj(h, norm1_g[l][None, :], wabc, wqk, wvt, wf, bf, tri, sel, pool_conv, B)
        o = _attention(q, k, vt, B)
        h = _mixer_out(h, norm1_g[l][None, :], ya, yb, o, wg, w_out_a[l].astype(BF16),
                       w_out_b[l].astype(BF16), w_out_c[l].astype(BF16), w_o[l].astype(BF16))

        wr2 = _hi_lo_weight(jnp.concatenate([router_g[l], router_e[l]], axis=1))
        rb = _pad_lanes(jnp.concatenate([router_g_b[l], router_e_b[l]])[None, :])
        grouped = lambda w: (w.reshape(N_GROUPS, EPG, D, EXPERT_HIDDEN).transpose(0, 2, 1, 3)
                             .reshape(N_GROUPS, D, GROUP_HIDDEN).astype(BF16))
        xs, cs, info, seg = _moe_sort(h, norm2_g[l][None, :], wr2, rb, tri_u)
        h = _moe_ffn(seg.reshape(-1), xs, cs, info, h, grouped(exp_w1[l]), grouped(exp_w3[l]),
                     exp_w2[l].reshape(N_GROUPS, GROUP_HIDDEN, D).astype(BF16))

    assert L % TM == 0
    return _final_norm(h, final_g[None, :], B, L).reshape(B, L, D)
```

```python
import functools
import math

import jax
import jax.numpy as jnp
from jax import lax
from jax.experimental import pallas as pl
from jax.experimental.pallas import tpu as pltpu

D = 1024
N_META = 16
EPS = 1e-6
POOL_W = 256
POOL_WINDOWS = (2, 4, 8, 16)
CONV_W = 256
CONV_K = 31
HEADS = 8
HEAD_DIM = 64
FOX_W = HEADS * HEAD_DIM
N_GROUPS = 4
EPG = 4
N_EXPERTS = 16
EXPERT_HIDDEN = 256
GROUP_HIDDEN = EPG * EXPERT_HIDDEN

LANES = 128
HEAD_PAD = LANES
ATT_W = HEADS * HEAD_PAD
LP = 8704
TM = 512
BLOCKS_PER_SEQ = LP // TM
HALO = 32
TQ = 512
N_FPIECE = 3
VMEM_LIMIT = 56 * 1024 * 1024
MOE_VMEM_LIMIT = 60 * 1024 * 1024
LOG2E = math.log2(math.e)

F32 = jnp.float32
BF16 = jnp.bfloat16


def _dot(a, b):
    return jnp.dot(a, b, preferred_element_type=F32)


def _dot_nt(a, b):
    return lax.dot_general(a, b, (((1,), (1,)), ((), ())), preferred_element_type=F32)


def _rms_norm(x, g):
    ms = jnp.mean(x * x, axis=-1, keepdims=True)
    return x * lax.rsqrt(ms + EPS) * g


def _split_bf16(x):
    hi = x.astype(BF16)
    lo = (x - hi.astype(F32)).astype(BF16)
    return hi, lo


CONV_CHUNK = 64
PH_ROWS = TM + HALO - 8


def _in_proj_kernel(h_ref, g_ref, wabc_ref, wqk_ref, wvt_ref, wf_ref, bf_ref, tri_ref, sel_ref,
                    pw_ref, pb_ref, ps_ref, cw_ref, cb_ref, lg_ref, lb_ref,
                    ya_ref, yb_ref, q_ref, k_ref, vt_ref, carry_ref, xa_ref, xg_ref, ph_ref):
    i = pl.program_id(0)
    blk = i % BLOCKS_PER_SEQ

    @pl.when(blk == 0)
    def _():
        carry_ref[...] = jnp.zeros_like(carry_ref)
        xa_ref[0:HALO, :] = jnp.zeros((HALO, POOL_W), F32)
        xg_ref[0:HALO, :] = jnp.zeros((HALO, CONV_W), F32)

    u = _rms_norm(h_ref[...], g_ref[...]).astype(BF16)

    abc = _dot(u, wabc_ref[...])
    xa_ref[HALO:, :] = abc[:, :POOL_W]
    xg_ref[HALO:, :] = abc[:, POOL_W:POOL_W + CONV_W] * jax.nn.sigmoid(abc[:, POOL_W + CONV_W:])

    pos = blk * TM + lax.broadcasted_iota(jnp.int32, (TM, LANES), 0) + 1
    lane = lax.broadcasted_iota(jnp.int32, (TM, LANES), 1)
    low = lane < (LANES // 2)
    means = []
    for t in range(2):
        x = xa_ref[:, t * LANES:(t + 1) * LANES]
        s2 = x + pltpu.roll(x, 1, 0)
        s4 = s2 + pltpu.roll(s2, 2, 0)
        if t == 0:
            small, big = s2, s4
        else:
            s8 = s4 + pltpu.roll(s4, 4, 0)
            small, big = s8, s8 + pltpu.roll(s8, 8, 0)
        w_small, w_big = POOL_WINDOWS[2 * t], POOL_WINDOWS[2 * t + 1]
        cnt = jnp.minimum(pos, jnp.where(low, w_small, w_big)).astype(F32)
        means.append(jnp.where(low, small[HALO:], big[HALO:]) / cnt - x[HALO:])
    diff = jnp.concatenate(means, axis=1).astype(BF16)
    ya_ref[...] = ((_dot(diff, pw_ref[...]) + pb_ref[...]) * ps_ref[...]).astype(BF16)

    for r in range(1, 8):
        ph_ref[r - 1] = xg_ref[r:r + PH_ROWS, :]
    cw = cw_ref[...]
    base = HALO - (CONV_K - 1)

    def conv_rows(c0):
        acc = jnp.zeros((CONV_CHUNK, CONV_W), F32) + cb_ref[...]
        for j in range(CONV_K):
            off = base + j
            r = off % 8
            rows = pl.ds(c0 + (off - r), CONV_CHUNK)
            tap = xg_ref[rows, :] if r == 0 else ph_ref[r - 1, rows, :]
            acc = acc + tap * cw[j:j + 1, :]
        mu = jnp.mean(acc, axis=-1, keepdims=True)
        cen = acc - mu
        var = jnp.mean(cen * cen, axis=-1, keepdims=True)
        y = cen * lax.rsqrt(var + EPS) * lg_ref[...] + lb_ref[...]
        yb_ref[c0:c0 + CONV_CHUNK, :] = (y * jax.nn.sigmoid(y)).astype(BF16)

    fg = _dot(u, wf_ref[...]) + bf_ref[...]
    logf = jnp.minimum(fg, 0.0) - jnp.log1p(jnp.exp(-jnp.abs(fg)))
    l_hi, l_lo = _split_bf16(logf)
    cum2 = _dot(tri_ref[...], jnp.concatenate([l_hi, l_lo], axis=1))
    cum = cum2[:, :LANES] + cum2[:, LANES:] + carry_ref[...]
    carry_ref[...] = cum[TM - 1:TM, :]

    lane = lax.broadcasted_iota(jnp.int32, (TM, LANES), 1)
    fv = jnp.where(lane < HEADS, cum * LOG2E, 0.0)
    f_hi = fv.astype(BF16).astype(F32)
    r1 = fv - f_hi
    f_mid = r1.astype(BF16).astype(F32)
    f_lo = (r1 - f_mid).astype(BF16).astype(F32)
    fcat = f_hi + pltpu.roll(f_mid, HEADS, 1) + pltpu.roll(f_lo, 2 * HEADS, 1)
    fcat = jnp.where(lane == N_FPIECE * HEADS, 1.0, fcat).astype(BF16)

    n_chunks = TM // CONV_CHUNK
    qk_chunk = 2 * ATT_W // n_chunks
    for n in range(n_chunks):
        cols = slice(n * qk_chunk, (n + 1) * qk_chunk)
        qk = (_dot(u, wqk_ref[:, cols]) + _dot(fcat, sel_ref[:, cols])).astype(BF16)
        if n < n_chunks // 2:
            q_ref[:, cols] = qk
        else:
            k_ref[:, n * qk_chunk - ATT_W:(n + 1) * qk_chunk - ATT_W] = qk
        conv_rows(n * CONV_CHUNK)

    xa_ref[0:HALO, :] = xa_ref[TM:TM + HALO, :]
    xg_ref[0:HALO, :] = xg_ref[TM:TM + HALO, :]

    vt = _dot_nt(wvt_ref[...], u)
    row = lax.broadcasted_iota(jnp.int32, (ATT_W, TM), 0)
    vt_ref[0] = jnp.where(row % HEAD_PAD == HEAD_DIM, 1.0, vt).astype(BF16)


def _in_proj(h, g, wabc, wqk, wvt, wf, bf, tri, sel, pool_conv, batch):
    tp = h.shape[0]
    nblk = tp // TM
    row = lambda w: pl.BlockSpec((TM, w), lambda i: (i, 0))
    const = lambda x: pl.BlockSpec(x.shape, lambda i: (0, 0))
    return pl.pallas_call(
        _in_proj_kernel,
        grid=(nblk,),
        in_specs=[row(D), const(g), const(wabc), const(wqk), const(wvt), const(wf), const(bf),
                  const(tri), const(sel)] + [const(p) for p in pool_conv],
        out_specs=[row(POOL_W), row(CONV_W), row(ATT_W), row(ATT_W),
                   pl.BlockSpec((1, ATT_W, TM), lambda i: (i // BLOCKS_PER_SEQ, 0, i % BLOCKS_PER_SEQ))],
        out_shape=[jax.ShapeDtypeStruct((tp, POOL_W), BF16), jax.ShapeDtypeStruct((tp, CONV_W), BF16),
                   jax.ShapeDtypeStruct((tp, ATT_W), BF16), jax.ShapeDtypeStruct((tp, ATT_W), BF16),
                   jax.ShapeDtypeStruct((batch, ATT_W, LP), BF16)],
        scratch_shapes=[pltpu.VMEM((1, LANES), F32),
                        pltpu.VMEM((TM + HALO, POOL_W), F32), pltpu.VMEM((TM + HALO, CONV_W), F32),
                        pltpu.VMEM((7, PH_ROWS, CONV_W), F32)],
        compiler_params=pltpu.CompilerParams(dimension_semantics=("arbitrary",),
                                             vmem_limit_bytes=VMEM_LIMIT),
        name="in_proj",
    )(h, g, wabc, wqk, wvt, wf, bf, tri, sel, *pool_conv)


HEADS_PER_STEP = 4


def _attn_kernel(q_ref, k_ref, vt_ref, o_ref, m_ref, acc_ref, s_ref, mt_ref):
    qi = pl.program_id(2)
    m_ref[...] = jnp.full_like(m_ref, -jnp.inf)
    acc_ref[...] = jnp.zeros_like(acc_ref)

    def scores(j, slot, masked=False):
        start = pl.multiple_of(j * TQ, TQ)
        for hd in range(HEADS_PER_STEP):
            cols = slice(hd * HEAD_PAD, (hd + 1) * HEAD_PAD)
            s = _dot_nt(k_ref[pl.ds(start, TQ), cols], q_ref[:, cols])
            if masked:
                key = lax.broadcasted_iota(jnp.int32, (TQ, TQ), 0)
                qry = lax.broadcasted_iota(jnp.int32, (TQ, TQ), 1)
                s = jnp.where(key <= qry, s, -jnp.inf)
            s_ref[slot, hd] = s
            mt_ref[slot, hd] = jnp.max(s, axis=0, keepdims=True)

    def values(j, slot):
        start = pl.multiple_of(j * TQ, TQ)
        for hd in range(HEADS_PER_STEP):
            cols = slice(hd * HEAD_PAD, (hd + 1) * HEAD_PAD)
            m_old = m_ref[hd]
            m_new = jnp.maximum(m_old, mt_ref[slot, hd])
            p = jnp.exp2(s_ref[slot, hd] - m_new).astype(BF16)
            alpha = jnp.exp2(m_old - m_new)
            pv = _dot(vt_ref[0, cols, pl.ds(start, TQ)], p)
            acc_ref[hd] = alpha * acc_ref[hd] + pv
            m_ref[hd] = m_new

    @pl.when(qi == 0)
    def _():
        scores(0, 0, masked=True)
        values(0, 0)

    @pl.when(qi > 0)
    def _():
        scores(0, 0)

        def two_tiles(j):
            scores(j + 1, 1)
            values(j, 0)
            scores(j + 2, 0)
            values(j + 1, 1)

        def body(t, c):
            two_tiles(4 * t)
            two_tiles(4 * t + 2)
            return c

        last = qi - 1
        n_pairs = last // 2
        lax.fori_loop(0, n_pairs // 2, body, 0)

        @pl.when(n_pairs % 2 == 1)
        def _():
            two_tiles(2 * (n_pairs - 1))

        @pl.when(last % 2 == 0)
        def _():
            scores(qi, 1, masked=True)
            values(last, 0)
            values(qi, 1)

        @pl.when(last % 2 == 1)
        def _():
            scores(last, 1)
            values(last - 1, 0)
            scores(qi, 0, masked=True)
            values(last, 1)
            values(qi, 0)

    outs = []
    for hd in range(HEADS_PER_STEP):
        acc = acc_ref[hd]
        outs.append(acc[:HEAD_DIM, :] / acc[HEAD_DIM:HEAD_DIM + 1, :])
    o_ref[...] = jnp.concatenate(outs, axis=0).T.astype(BF16)


def _attention(q, k, vt, batch):
    nq = LP // TQ
    w = HEADS_PER_STEP * HEAD_PAD
    return pl.pallas_call(
        _attn_kernel,
        grid=(batch, HEADS // HEADS_PER_STEP, nq),
        in_specs=[pl.BlockSpec((TQ, w), lambda b, h, i: (b * nq + i, h)),
                  pl.BlockSpec((LP, w), lambda b, h, i: (b, h)),
                  pl.BlockSpec((1, w, LP), lambda b, h, i: (b, h, 0))],
        out_specs=pl.BlockSpec((TQ, HEADS_PER_STEP * HEAD_DIM), lambda b, h, i: (b * nq + i, h)),
        out_shape=jax.ShapeDtypeStruct((batch * LP, FOX_W), BF16),
        scratch_shapes=[pltpu.VMEM((HEADS_PER_STEP, 1, TQ), F32),
                        pltpu.VMEM((HEADS_PER_STEP, HEAD_PAD, TQ), F32),
                        pltpu.VMEM((2, HEADS_PER_STEP, TQ, TQ), F32),
                        pltpu.VMEM((2, HEADS_PER_STEP, 1, TQ), F32)],
        compiler_params=pltpu.CompilerParams(dimension_semantics=("arbitrary", "arbitrary", "arbitrary"),
                                             vmem_limit_bytes=VMEM_LIMIT),
        name="fox_attention",
    )(q, k, vt)


def _mixer_out_kernel(h_ref, g_ref, ya_ref, yb_ref, o_ref, wg_ref,
                      woa_ref, wob_ref, woc_ref, wo_ref, out_ref):
    h = h_ref[...]
    u = _rms_norm(h, g_ref[...]).astype(BF16)
    branches = (_dot(ya_ref[...], woa_ref[...]), _dot(yb_ref[...], wob_ref[...]),
                _dot(o_ref[...], woc_ref[...]))
    merged = None
    for n, y in enumerate(branches):
        gated = jax.nn.sigmoid(_dot(u, wg_ref[:, n * D:(n + 1) * D])) * y
        merged = gated if merged is None else merged + gated
    out_ref[...] = h + _dot(merged.astype(BF16), wo_ref[...])


def _mixer_out(h, g, ya, yb, o, wg, woa, wob, woc, wo):
    tp = h.shape[0]
    nblk = tp // TM
    row = lambda w: pl.BlockSpec((TM, w), lambda i: (i, 0))
    const = lambda x: pl.BlockSpec(x.shape, lambda i: (0, 0))
    return pl.pallas_call(
        _mixer_out_kernel,
        grid=(nblk,),
        in_specs=[row(D), const(g), row(POOL_W), row(CONV_W), row(FOX_W),
                  const(wg), const(woa), const(wob), const(woc), const(wo)],
        out_specs=row(D),
        out_shape=jax.ShapeDtypeStruct((tp, D), F32),
        compiler_params=pltpu.CompilerParams(dimension_semantics=("arbitrary",),
                                             vmem_limit_bytes=VMEM_LIMIT),
        name="mixer_out",
    )(h, g, ya, yb, o, wg, woa, wob, woc, wo)


TS = 1024
SEG = 320
SEG_ALIGN = 64
SLOT_LANE = EPG


def _route(lt):
    gl = [lt[g:g + 1, :] for g in range(N_GROUPS)]
    gmax = functools.reduce(jnp.maximum, gl)
    gsum = functools.reduce(lambda x, y: x + y, [jnp.exp(x - gmax) for x in gl])
    gw = 1.0 / gsum
    sel, taken = [], jnp.zeros_like(gmax, dtype=jnp.bool_)
    for g in range(N_GROUPS):
        s = jnp.logical_and(gl[g] == gmax, jnp.logical_not(taken))
        sel.append(s)
        taken = jnp.logical_or(taken, s)
    el = []
    for j in range(EPG):
        e = jnp.zeros_like(gmax)
        for g in range(N_GROUPS):
            e = jnp.where(sel[g], lt[N_GROUPS + g * EPG + j:N_GROUPS + g * EPG + j + 1, :], e)
        el.append(e)
    neg = jnp.full_like(gmax, -jnp.inf)
    top1 = functools.reduce(jnp.maximum, el)
    is1, taken = [], jnp.zeros_like(gmax, dtype=jnp.bool_)
    for j in range(EPG):
        s = jnp.logical_and(el[j] == top1, jnp.logical_not(taken))
        is1.append(s)
        taken = jnp.logical_or(taken, s)
    rest = [jnp.where(is1[j], neg, el[j]) for j in range(EPG)]
    top2 = functools.reduce(jnp.maximum, rest)
    is2, taken = [], jnp.zeros_like(gmax, dtype=jnp.bool_)
    for j in range(EPG):
        s = jnp.logical_and(jnp.logical_and(rest[j] == top2, jnp.logical_not(is1[j])),
                            jnp.logical_not(taken))
        is2.append(s)
        taken = jnp.logical_or(taken, s)
    w1 = 1.0 / (1.0 + jnp.exp(top2 - top1))
    w2 = 1.0 - w1
    comb = [jnp.where(is1[j], w1, jnp.where(is2[j], w2, 0.0)) * gw for j in range(EPG)]
    return sel, comb


def _moe_sort_kernel(h_ref, g_ref, wr2_ref, rb_ref, tri_ref, xs_ref, cs_ref, info_ref, seg_ref):
    vn = _rms_norm(h_ref[...], g_ref[...])
    v_hi, v_lo = _split_bf16(vn)
    wr2 = wr2_ref[...]
    r2 = _dot(v_hi, wr2)
    logits = r2[:, :LANES] + r2[:, LANES:] + _dot(v_lo, wr2[:, :LANES]) + rb_ref[...]
    sel, comb = _route(logits.T)

    onehot = jnp.concatenate([s.astype(F32) for s in sel] + [jnp.zeros((8 - N_GROUPS, TS), F32)], axis=0)
    cnt = _dot(onehot.astype(BF16), tri_ref[...])
    totals = [jnp.max(cnt[g:g + 1, :], axis=1, keepdims=True) for g in range(N_GROUPS)]
    starts = [jnp.zeros((1, 1), F32)]
    for g in range(N_GROUPS - 1):
        starts.append(starts[g] + totals[g])
    slot = jnp.zeros((1, TS), F32)
    for g in range(N_GROUPS):
        slot = jnp.where(sel[g], starts[g] + cnt[g:g + 1, :] - 1.0, slot)

    info = jnp.concatenate(comb + [slot, jnp.zeros((LANES - EPG - 1, TS), F32)], axis=0).T
    info_ref[...] = info

    srow = lax.broadcasted_iota(jnp.int32, (TS, TS), 0)
    perm = jnp.where(srow == slot.astype(jnp.int32), 1.0, 0.0).astype(BF16)
    lane = lax.broadcasted_iota(jnp.int32, (TS, LANES), 1)
    c_hi = info.astype(BF16).astype(F32)
    c_lo = (info - c_hi).astype(BF16).astype(F32)
    pack = jnp.where(lane < EPG, c_hi, pltpu.roll(c_lo, EPG, 1)).astype(BF16)
    xs_ext = _dot(perm, jnp.concatenate([v_hi, pack], axis=1))
    xs_ref[...] = xs_ext[:, :D].astype(BF16)
    cs = xs_ext[:, D:]
    cs_ref[...] = cs + pltpu.roll(cs, LANES - EPG, 1)

    lane1 = lax.broadcasted_iota(jnp.int32, (1, LANES), 1)
    seg = jnp.zeros((1, LANES), F32)
    for g in range(N_GROUPS):
        seg = jnp.where(lane1 == g, starts[g], seg)
        seg = jnp.where(lane1 == N_GROUPS + g, starts[g] + totals[g], seg)
    seg_ref[0] = seg.astype(jnp.int32)


def _moe_sort(h, g, wr2, rb, tri):
    tp = h.shape[0]
    nblk = tp // TS
    const = lambda x: pl.BlockSpec(x.shape, lambda i: (0, 0))
    row = lambda w: pl.BlockSpec((TS, w), lambda i: (i, 0))
    return pl.pallas_call(
        _moe_sort_kernel,
        grid=(nblk,),
        in_specs=[row(D), const(g), const(wr2), const(rb), const(tri)],
        out_specs=[row(D), row(LANES), row(LANES), pl.BlockSpec((1, 1, LANES), lambda i: (i, 0, 0))],
        out_shape=[jax.ShapeDtypeStruct((tp, D), BF16), jax.ShapeDtypeStruct((tp, LANES), F32),
                   jax.ShapeDtypeStruct((tp, LANES), F32), jax.ShapeDtypeStruct((nblk, 1, LANES), jnp.int32)],
        compiler_params=pltpu.CompilerParams(dimension_semantics=("arbitrary",),
                                             vmem_limit_bytes=VMEM_LIMIT),
        name="moe_sort",
    )(h, g, wr2, rb, tri)


def _moe_ffn_kernel(seg_ref, xs_ref, cs_ref, info_ref, h_ref, w1_ref, w3_ref, w2_ref, out_ref, ys_ref):
    i = pl.program_id(0)
    ys_ref[...] = jnp.zeros_like(ys_ref)

    def group(g, carry):
        start = seg_ref[i * LANES + g]
        end = seg_ref[i * LANES + N_GROUPS + g]
        first = (start // SEG_ALIGN) * SEG_ALIGN
        n_win = jnp.where(end > start, (end - first + SEG - 1) // SEG, 0)

        def window(w, c2):
            lo = first + w * SEG
            ws = pl.multiple_of(jnp.minimum(lo, TS - SEG), SEG_ALIGN)
            rows = pl.ds(ws, SEG)
            x = xs_ref[rows, :]
            h1 = _dot(x, w1_ref[g])
            h3 = _dot(x, w3_ref[g])
            he = h1 * jax.nn.sigmoid(h1) * h3
            r = ws + lax.broadcasted_iota(jnp.int32, (SEG, 1), 0)
            inseg = jnp.logical_and(r >= jnp.maximum(start, lo), r < jnp.minimum(end, lo + SEG))
            cs = cs_ref[rows, :]
            parts = []
            for j in range(EPG):
                wcol = jnp.where(inseg, cs[:, j:j + 1], 0.0)
                parts.append((he[:, j * EXPERT_HIDDEN:(j + 1) * EXPERT_HIDDEN] * wcol).astype(BF16))
            ys_ref[rows, :] += _dot(jnp.concatenate(parts, axis=1), w2_ref[g])
            return c2

        lax.fori_loop(0, n_win, window, 0)
        return carry

    lax.fori_loop(0, N_GROUPS, group, 0)

    slot = info_ref[:, SLOT_LANE:SLOT_LANE + 1].astype(jnp.int32)
    scol = lax.broadcasted_iota(jnp.int32, (TS, TS), 1)
    unperm = jnp.where(scol == slot, 1.0, 0.0).astype(BF16)
    out_ref[...] = h_ref[...] + _dot(unperm, ys_ref[...].astype(BF16))


def _moe_ffn(seg, xs, cs, info, h, w1g, w3g, w2g):
    tp = h.shape[0]
    nblk = tp // TS
    row = lambda w: pl.BlockSpec((TS, w), lambda i, s: (i, 0))
    resident = lambda x: pl.BlockSpec(x.shape, lambda i, s: (0, 0, 0), pipeline_mode=pl.Buffered(1))
    return pl.pallas_call(
        _moe_ffn_kernel,
        grid_spec=pltpu.PrefetchScalarGridSpec(
            num_scalar_prefetch=1,
            grid=(nblk,),
            in_specs=[row(D), row(LANES), row(LANES), row(D), resident(w1g), resident(w3g), resident(w2g)],
            out_specs=row(D),
            scratch_shapes=[pltpu.VMEM((TS, D), F32)]),
        out_shape=jax.ShapeDtypeStruct((tp, D), F32),
        compiler_params=pltpu.CompilerParams(dimension_semantics=("arbitrary",),
                                             vmem_limit_bytes=MOE_VMEM_LIMIT),
        name="moe_ffn",
    )(seg, xs, cs, info, h, w1g, w3g, w2g)


def _final_norm_kernel(h_ref, g_ref, out_ref):
    out_ref[...] = _rms_norm(h_ref[...], g_ref[...])


def _final_norm(h, g, batch, seq_out):
    nblk = seq_out // TM
    return pl.pallas_call(
        _final_norm_kernel,
        grid=(batch, nblk),
        in_specs=[pl.BlockSpec((pl.Element(TM), pl.Element(D)),
                               lambda b, r: (pl.multiple_of(b * LP + N_META + r * TM, N_META), 0)),
                  pl.BlockSpec((1, D), lambda b, r: (0, 0))],
        out_specs=pl.BlockSpec((TM, D), lambda b, r: (b * nblk + r, 0)),
        out_shape=jax.ShapeDtypeStruct((batch * seq_out, D), F32),
        compiler_params=pltpu.CompilerParams(dimension_semantics=("arbitrary", "arbitrary")),
        name="final_norm",
    )(h, g)


def _pad_lanes(x, n=LANES):
    return jnp.pad(x, ((0, 0), (0, n - x.shape[1])))


def _pad_rows(w, n):
    return jnp.pad(w, ((0, n - w.shape[0]), (0, 0)))


def _hi_lo_weight(w):
    hi = w.astype(BF16)
    lo = (w - hi.astype(F32)).astype(BF16)
    return jnp.concatenate([_pad_lanes(hi), _pad_lanes(lo)], axis=1)


def _per_head_cols(w):
    k = w.shape[0]
    w = w.reshape(k, HEADS, HEAD_DIM)
    return jnp.pad(w, ((0, 0), (0, 0), (0, HEAD_PAD - HEAD_DIM))).reshape(k, ATT_W)


def _bias_select():
    sel = jnp.zeros((LANES, 2, HEADS, HEAD_PAD), F32)
    one_lane = N_FPIECE * HEADS
    for p in range(N_FPIECE):
        for hd in range(HEADS):
            sel = sel.at[p * HEADS + hd, 0, hd, HEAD_DIM + p].set(1.0)
            sel = sel.at[p * HEADS + hd, 1, hd, HEAD_DIM + N_FPIECE + p].set(-1.0)
    sel = sel.at[one_lane, 0, :, HEAD_DIM + N_FPIECE:HEAD_DIM + 2 * N_FPIECE].set(1.0)
    sel = sel.at[one_lane, 1, :, HEAD_DIM:HEAD_DIM + N_FPIECE].set(1.0)
    return sel.reshape(LANES, 2 * ATT_W).astype(BF16)


def kernel(x, meta, norm1_g, w_in, b_forget, pool_w, pool_b, pool_scale, conv_w, conv_b, conv_ln_g,
           conv_ln_b, w_out_a, w_out_b, w_out_c, w_o, norm2_g, router_g, router_g_b, router_e,
           router_e_b, exp_w1, exp_w3, exp_w2, final_g):
    B, L, _ = x.shape
    depth = w_in.shape[0]
    seq = N_META + L
    assert seq <= LP and LP % TM == 0 and LP % TQ == 0 and (B * LP) % TS == 0
    m = jnp.broadcast_to(meta.astype(x.dtype)[None], (B, N_META, D))
    h = jnp.concatenate([m, x, jnp.zeros((B, LP - seq, D), x.dtype)], axis=1).reshape(B * LP, D)
    tri = jnp.tril(jnp.ones((TM, TM), BF16))
    tri_u = jnp.triu(jnp.ones((TS, TS), BF16))
    sel = _bias_select()

    c_abc = POOL_W + 2 * CONV_W
    c_q = c_abc + FOX_W
    c_k = c_q + FOX_W
    c_v = c_k + FOX_W
    c_f = c_v + HEADS
    for l in range(depth):
        wl = w_in[l]
        wabc = wl[:, :c_abc].astype(BF16)
        wq = _per_head_cols(wl[:, c_abc:c_q] * (HEAD_DIM ** -0.5 * LOG2E))
        wk = _per_head_cols(wl[:, c_q:c_k])
        wqk = jnp.concatenate([wq, wk], axis=1).astype(BF16)
        wvt = _per_head_cols(wl[:, c_k:c_v]).T.astype(BF16)
        wf = _pad_lanes(wl[:, c_v:c_f]).astype(BF16)
        wg = wl[:, c_f:].astype(BF16)
        bf = _pad_lanes(b_forget[l][None, :])
        pw = jax.scipy.linalg.block_diag(*[pool_w[l, g] for g in range(pool_w.shape[1])]).astype(BF16)
        pool_conv = (pw, pool_b[l].reshape(1, POOL_W), pool_scale[l][None, :], _pad_rows(conv_w[l], 32),
                     conv_b[l][None, :], conv_ln_g[l][None, :], conv_ln_b[l][None, :])
        ya, yb, q, k, vt = _in_proj(h, norm1_g[l][None, :], wabc, wqk, wvt, wf, bf, tri, sel, pool_conv, B)
        o = _attention(q, k, vt, B)
        h = _mixer_out(h, norm1_g[l][None, :], ya, yb, o, wg, w_out_a[l].astype(BF16),
                       w_out_b[l].astype(BF16), w_out_c[l].astype(BF16), w_o[l].astype(BF16))

        wr2 = _hi_lo_weight(jnp.concatenate([router_g[l], router_e[l]], axis=1))
        rb = _pad_lanes(jnp.concatenate([router_g_b[l], router_e_b[l]])[None, :])
        grouped = lambda w: (w.reshape(N_GROUPS, EPG, D, EXPERT_HIDDEN).transpose(0, 2, 1, 3)
                             .reshape(N_GROUPS, D, GROUP_HIDDEN).astype(BF16))
        xs, cs, info, seg = _moe_sort(h, norm2_g[l][None, :], wr2, rb, tri_u)
        h = _moe_ffn(seg.reshape(-1), xs, cs, info, h, grouped(exp_w1[l]), grouped(exp_w3[l]),
                     exp_w2[l].reshape(N_GROUPS, GROUP_HIDDEN, D).astype(BF16))

    assert L % TM == 0
    return _final_norm(h, final_g[None, :], B, L).reshape(B, L, D)
```

```python
import functools
import math

import jax
import jax.numpy as jnp
from jax import lax
from jax.experimental import pallas as pl
from jax.experimental.pallas import tpu as pltpu

D = 1024
N_META = 16
EPS = 1e-6
POOL_W = 256
POOL_WINDOWS = (2, 4, 8, 16)
CONV_W = 256
CONV_K = 31
HEADS = 8
HEAD_DIM = 64
FOX_W = HEADS * HEAD_DIM
N_GROUPS = 4
EPG = 4
N_EXPERTS = 16
EXPERT_HIDDEN = 256
GROUP_HIDDEN = EPG * EXPERT_HIDDEN

LANES = 128
HEAD_PAD = LANES
ATT_W = HEADS * HEAD_PAD
LP = 8704
TM = 512
BLOCKS_PER_SEQ = LP // TM
HALO = 32
TQ = 512
N_FPIECE = 3
VMEM_LIMIT = 56 * 1024 * 1024
MOE_VMEM_LIMIT = 60 * 1024 * 1024
LOG2E = math.log2(math.e)

F32 = jnp.float32
BF16 = jnp.bfloat16


def _dot(a, b):
    return jnp.dot(a, b, preferred_element_type=F32)


def _dot_nt(a, b):
    return lax.dot_general(a, b, (((1,), (1,)), ((), ())), preferred_element_type=F32)


def _rms_norm(x, g):
    ms = jnp.mean(x * x, axis=-1, keepdims=True)
    return x * lax.rsqrt(ms + EPS) * g


def _split_bf16(x):
    hi = x.astype(BF16)
    lo = (x - hi.astype(F32)).astype(BF16)
    return hi, lo


CONV_CHUNK = 64
PH_ROWS = TM + HALO - 8


def _in_proj_kernel(first_layer, *refs):
    if first_layer:
        x_ref, meta_ref, *refs = refs
    else:
        h_ref, *refs = refs
    (g_ref, wabc_ref, wqk_ref, wvt_ref, wf_ref, bf_ref, tri_ref, sel_ref,
     pw_ref, pb_ref, ps_ref, cw_ref, cb_ref, lg_ref, lb_ref,
     ya_ref, yb_ref, q_ref, k_ref, vt_ref, *refs) = refs
    if first_layer:
        h_ref, *refs = refs
    carry_ref, xa_ref, xg_ref, ph_ref = refs
    i = pl.program_id(0)
    blk = i % BLOCKS_PER_SEQ

    if first_layer:
        @pl.when(blk == 0)
        def _():
            h_ref[0:N_META, :] = meta_ref[...]
            h_ref[N_META:, :] = x_ref[0:TM - N_META, :]

        @pl.when(blk == BLOCKS_PER_SEQ - 1)
        def _():
            h_ref[0:N_META, :] = x_ref[TM - N_META:, :]
            h_ref[N_META:, :] = jnp.zeros((TM - N_META, D), F32)

        @pl.when(jnp.logical_and(blk > 0, blk < BLOCKS_PER_SEQ - 1))
        def _():
            h_ref[...] = x_ref[...]

    @pl.when(blk == 0)
    def _():
        carry_ref[...] = jnp.zeros_like(carry_ref)
        xa_ref[0:HALO, :] = jnp.zeros((HALO, POOL_W), F32)
        xg_ref[0:HALO, :] = jnp.zeros((HALO, CONV_W), F32)

    u = _rms_norm(h_ref[...], g_ref[...]).astype(BF16)

    abc = _dot(u, wabc_ref[...])
    xa_ref[HALO:, :] = abc[:, :POOL_W]
    xg_ref[HALO:, :] = abc[:, POOL_W:POOL_W + CONV_W] * jax.nn.sigmoid(abc[:, POOL_W + CONV_W:])

    pos = blk * TM + lax.broadcasted_iota(jnp.int32, (TM, LANES), 0) + 1
    lane = lax.broadcasted_iota(jnp.int32, (TM, LANES), 1)
    low = lane < (LANES // 2)
    means = []
    for t in range(2):
        x = xa_ref[:, t * LANES:(t + 1) * LANES]
        s2 = x + pltpu.roll(x, 1, 0)
        s4 = s2 + pltpu.roll(s2, 2, 0)
        if t == 0:
            small, big = s2, s4
        else:
            s8 = s4 + pltpu.roll(s4, 4, 0)
            small, big = s8, s8 + pltpu.roll(s8, 8, 0)
        w_small, w_big = POOL_WINDOWS[2 * t], POOL_WINDOWS[2 * t + 1]
        cnt = jnp.minimum(pos, jnp.where(low, w_small, w_big)).astype(F32)
        means.append(jnp.where(low, small[HALO:], big[HALO:]) / cnt - x[HALO:])
    diff = jnp.concatenate(means, axis=1).astype(BF16)
    ya_ref[...] = ((_dot(diff, pw_ref[...]) + pb_ref[...]) * ps_ref[...]).astype(BF16)

    for r in range(1, 8):
        ph_ref[r - 1] = xg_ref[r:r + PH_ROWS, :]
    cw = cw_ref[...]
    base = HALO - (CONV_K - 1)

    def conv_rows(c0):
        acc = jnp.zeros((CONV_CHUNK, CONV_W), F32) + cb_ref[...]
        for j in range(CONV_K):
            off = base + j
            r = off % 8
            rows = pl.ds(c0 + (off - r), CONV_CHUNK)
            tap = xg_ref[rows, :] if r == 0 else ph_ref[r - 1, rows, :]
            acc = acc + tap * cw[j:j + 1, :]
        mu = jnp.mean(acc, axis=-1, keepdims=True)
        cen = acc - mu
        var = jnp.mean(cen * cen, axis=-1, keepdims=True)
        y = cen * lax.rsqrt(var + EPS) * lg_ref[...] + lb_ref[...]
        yb_ref[c0:c0 + CONV_CHUNK, :] = (y * jax.nn.sigmoid(y)).astype(BF16)

    fg = _dot(u, wf_ref[...]) + bf_ref[...]
    logf = jnp.minimum(fg, 0.0) - jnp.log1p(jnp.exp(-jnp.abs(fg)))
    l_hi, l_lo = _split_bf16(logf)
    cum2 = _dot(tri_ref[...], jnp.concatenate([l_hi, l_lo], axis=1))
    cum = cum2[:, :LANES] + cum2[:, LANES:] + carry_ref[...]
    carry_ref[...] = cum[TM - 1:TM, :]

    lane = lax.broadcasted_iota(jnp.int32, (TM, LANES), 1)
    fv = jnp.where(lane < HEADS, cum * LOG2E, 0.0)
    f_hi = fv.astype(BF16).astype(F32)
    r1 = fv - f_hi
    f_mid = r1.astype(BF16).astype(F32)
    f_lo = (r1 - f_mid).astype(BF16).astype(F32)
    fcat = f_hi + pltpu.roll(f_mid, HEADS, 1) + pltpu.roll(f_lo, 2 * HEADS, 1)
    fcat = jnp.where(lane == N_FPIECE * HEADS, 1.0, fcat).astype(BF16)

    for c0 in range(0, TM, CONV_CHUNK):
        conv_rows(c0)

    qkc = _dot(u, wqk_ref[...])
    bias_cols = _dot(fcat, sel_ref[...])
    from_proj = lane < HEAD_DIM
    for part, out_ref in enumerate((q_ref, k_ref)):
        for hd in range(HEADS):
            src = qkc[:, part * FOX_W + (hd // 2) * LANES:part * FOX_W + (hd // 2 + 1) * LANES]
            if hd % 2 == 1:
                src = pltpu.roll(src, HEAD_DIM, 1)
            tile = slice(hd * HEAD_PAD, (hd + 1) * HEAD_PAD)
            bias = bias_cols[:, part * ATT_W + hd * HEAD_PAD:part * ATT_W + (hd + 1) * HEAD_PAD]
            out_ref[:, tile] = jnp.where(from_proj, src, bias).astype(BF16)

    xa_ref[0:HALO, :] = xa_ref[TM:TM + HALO, :]
    xg_ref[0:HALO, :] = xg_ref[TM:TM + HALO, :]

    vt = _dot_nt(wvt_ref[...], u)
    row = lax.broadcasted_iota(jnp.int32, (ATT_W, TM), 0)
    vt_ref[0] = jnp.where(row % HEAD_PAD == HEAD_DIM, 1.0, vt).astype(BF16)


def _in_proj(src, g, wabc, wqk, wvt, wf, bf, tri, sel, pool_conv, batch):
    first_layer = isinstance(src, tuple)
    tp = batch * LP
    nblk = tp // TM
    row = lambda w: pl.BlockSpec((TM, w), lambda i: (i, 0))
    const = lambda x: pl.BlockSpec(x.shape, lambda i: (0, 0))
    if first_layer:
        x, meta = src
        seq_x = x.shape[0] // batch

        def x_rows(i):
            start = jnp.clip((i % BLOCKS_PER_SEQ) * TM - N_META, 0, seq_x - TM)
            return pl.multiple_of((i // BLOCKS_PER_SEQ) * seq_x + start, N_META), 0

        src_args = [x, meta]
        src_specs = [pl.BlockSpec((pl.Element(TM), pl.Element(D)), x_rows), const(meta)]
        h_spec, h_shape = [row(D)], [jax.ShapeDtypeStruct((tp, D), F32)]
    else:
        src_args, src_specs, h_spec, h_shape = [src], [row(D)], [], []
    return pl.pallas_call(
        functools.partial(_in_proj_kernel, first_layer),
        grid=(nblk,),
        in_specs=src_specs + [const(g), const(wabc), const(wqk), const(wvt), const(wf), const(bf),
                              const(tri), const(sel)] + [const(p) for p in pool_conv],
        out_specs=[row(POOL_W), row(CONV_W), row(ATT_W), row(ATT_W),
                   pl.BlockSpec((1, ATT_W, TM), lambda i: (i // BLOCKS_PER_SEQ, 0, i % BLOCKS_PER_SEQ))] + h_spec,
        out_shape=[jax.ShapeDtypeStruct((tp, POOL_W), BF16), jax.ShapeDtypeStruct((tp, CONV_W), BF16),
                   jax.ShapeDtypeStruct((tp, ATT_W), BF16), jax.ShapeDtypeStruct((tp, ATT_W), BF16),
                   jax.ShapeDtypeStruct((batch, ATT_W, LP), BF16)] + h_shape,
        scratch_shapes=[pltpu.VMEM((1, LANES), F32),
                        pltpu.VMEM((TM + HALO, POOL_W), F32), pltpu.VMEM((TM + HALO, CONV_W), F32),
                        pltpu.VMEM((7, PH_ROWS, CONV_W), F32)],
        compiler_params=pltpu.CompilerParams(dimension_semantics=("arbitrary",),
                                             vmem_limit_bytes=VMEM_LIMIT),
        name="in_proj",
    )(*src_args, g, wabc, wqk, wvt, wf, bf, tri, sel, *pool_conv)


HEADS_PER_STEP = 4


def _attn_kernel(q_ref, k_ref, vt_ref, o_ref, m_ref, acc_ref, s_ref, mt_ref):
    qi = pl.program_id(2)
    m_ref[...] = jnp.full_like(m_ref, -jnp.inf)
    acc_ref[...] = jnp.zeros_like(acc_ref)

    def scores(j, slot, masked=False):
        start = pl.multiple_of(j * TQ, TQ)
        for hd in range(HEADS_PER_STEP):
            cols = slice(hd * HEAD_PAD, (hd + 1) * HEAD_PAD)
            s = _dot_nt(k_ref[pl.ds(start, TQ), cols], q_ref[:, cols])
            if masked:
                key = lax.broadcasted_iota(jnp.int32, (TQ, TQ), 0)
                qry = lax.broadcasted_iota(jnp.int32, (TQ, TQ), 1)
                s = jnp.where(key <= qry, s, -jnp.inf)
            s_ref[slot, hd] = s
            mt_ref[slot, hd] = jnp.max(s, axis=0, keepdims=True)

    def values(j, slot):
        start = pl.multiple_of(j * TQ, TQ)
        for hd in range(HEADS_PER_STEP):
            cols = slice(hd * HEAD_PAD, (hd + 1) * HEAD_PAD)
            m_old = m_ref[hd]
            m_new = jnp.maximum(m_old, mt_ref[slot, hd])
            p = jnp.exp2(s_ref[slot, hd] - m_new).astype(BF16)
            alpha = jnp.exp2(m_old - m_new)
            pv = _dot(vt_ref[0, cols, pl.ds(start, TQ)], p)
            acc_ref[hd] = alpha * acc_ref[hd] + pv
            m_ref[hd] = m_new

    @pl.when(qi == 0)
    def _():
        scores(0, 0, masked=True)
        values(0, 0)

    @pl.when(qi > 0)
    def _():
        scores(0, 0)

        def two_tiles(j):
            scores(j + 1, 1)
            values(j, 0)
            scores(j + 2, 0)
            values(j + 1, 1)

        def body(t, c):
            two_tiles(4 * t)
            two_tiles(4 * t + 2)
            return c

        last = qi - 1
        n_pairs = last // 2
        lax.fori_loop(0, n_pairs // 2, body, 0)

        @pl.when(n_pairs % 2 == 1)
        def _():
            two_tiles(2 * (n_pairs - 1))

        @pl.when(last % 2 == 0)
        def _():
            scores(qi, 1, masked=True)
            values(last, 0)
            values(qi, 1)

        @pl.when(last % 2 == 1)
        def _():
            scores(last, 1)
            values(last - 1, 0)
            scores(qi, 0, masked=True)
            values(last, 1)
            values(qi, 0)

    outs = []
    for hd in range(HEADS_PER_STEP):
        acc = acc_ref[hd]
        outs.append(acc[:HEAD_DIM, :] / acc[HEAD_DIM:HEAD_DIM + 1, :])
    o_ref[...] = jnp.concatenate(outs, axis=0).T.astype(BF16)


def _attention(q, k, vt, batch):
    nq = LP // TQ
    w = HEADS_PER_STEP * HEAD_PAD
    return pl.pallas_call(
        _attn_kernel,
        grid=(batch, HEADS // HEADS_PER_STEP, nq),
        in_specs=[pl.BlockSpec((TQ, w), lambda b, h, i: (b * nq + i, h)),
                  pl.BlockSpec((LP, w), lambda b, h, i: (b, h)),
                  pl.BlockSpec((1, w, LP), lambda b, h, i: (b, h, 0))],
        out_specs=pl.BlockSpec((TQ, HEADS_PER_STEP * HEAD_DIM), lambda b, h, i: (b * nq + i, h)),
        out_shape=jax.ShapeDtypeStruct((batch * LP, FOX_W), BF16),
        scratch_shapes=[pltpu.VMEM((HEADS_PER_STEP, 1, TQ), F32),
                        pltpu.VMEM((HEADS_PER_STEP, HEAD_PAD, TQ), F32),
                        pltpu.VMEM((2, HEADS_PER_STEP, TQ, TQ), F32),
                        pltpu.VMEM((2, HEADS_PER_STEP, 1, TQ), F32)],
        compiler_params=pltpu.CompilerParams(dimension_semantics=("arbitrary", "arbitrary", "arbitrary"),
                                             vmem_limit_bytes=VMEM_LIMIT),
        name="fox_attention",
    )(q, k, vt)


TM_OUT = 1024


def _mixer_out_kernel(h_ref, g_ref, ya_ref, yb_ref, o_ref, wg_ref,
                      woa_ref, wob_ref, woc_ref, wo_ref, out_ref):
    h = h_ref[...]
    u = _rms_norm(h, g_ref[...]).astype(BF16)
    branches = (_dot(ya_ref[...], woa_ref[...]), _dot(yb_ref[...], wob_ref[...]),
                _dot(o_ref[...], woc_ref[...]))
    merged = None
    for n, y in enumerate(branches):
        gated = jax.nn.sigmoid(_dot(u, wg_ref[:, n * D:(n + 1) * D])) * y
        merged = gated if merged is None else merged + gated
    out_ref[...] = h + _dot(merged.astype(BF16), wo_ref[...])


def _mixer_out(h, g, ya, yb, o, wg, woa, wob, woc, wo):
    tp = h.shape[0]
    nblk = tp // TM_OUT
    row = lambda w: pl.BlockSpec((TM_OUT, w), lambda i: (i, 0))
    const = lambda x: pl.BlockSpec(x.shape, lambda i: (0, 0), pipeline_mode=pl.Buffered(1))
    return pl.pallas_call(
        _mixer_out_kernel,
        grid=(nblk,),
        in_specs=[row(D), const(g), row(POOL_W), row(CONV_W), row(FOX_W),
                  const(wg), const(woa), const(wob), const(woc), const(wo)],
        out_specs=row(D),
        out_shape=jax.ShapeDtypeStruct((tp, D), F32),
        compiler_params=pltpu.CompilerParams(dimension_semantics=("arbitrary",),
                                             vmem_limit_bytes=VMEM_LIMIT),
        name="mixer_out",
    )(h, g, ya, yb, o, wg, woa, wob, woc, wo)


TS = 1024
SEG = 320
SEG_ALIGN = 64
SLOT_LANE = EPG


def _route(lt):
    gl = [lt[g:g + 1, :] for g in range(N_GROUPS)]
    gmax = functools.reduce(jnp.maximum, gl)
    gsum = functools.reduce(lambda x, y: x + y, [jnp.exp(x - gmax) for x in gl])
    gw = 1.0 / gsum
    sel, taken = [], jnp.zeros_like(gmax, dtype=jnp.bool_)
    for g in range(N_GROUPS):
        s = jnp.logical_and(gl[g] == gmax, jnp.logical_not(taken))
        sel.append(s)
        taken = jnp.logical_or(taken, s)
    el = []
    for j in range(EPG):
        e = jnp.zeros_like(gmax)
        for g in range(N_GROUPS):
            e = jnp.where(sel[g], lt[N_GROUPS + g * EPG + j:N_GROUPS + g * EPG + j + 1, :], e)
        el.append(e)
    neg = jnp.full_like(gmax, -jnp.inf)
    top1 = functools.reduce(jnp.maximum, el)
    is1, taken = [], jnp.zeros_like(gmax, dtype=jnp.bool_)
    for j in range(EPG):
        s = jnp.logical_and(el[j] == top1, jnp.logical_not(taken))
        is1.append(s)
        taken = jnp.logical_or(taken, s)
    rest = [jnp.where(is1[j], neg, el[j]) for j in range(EPG)]
    top2 = functools.reduce(jnp.maximum, rest)
    is2, taken = [], jnp.zeros_like(gmax, dtype=jnp.bool_)
    for j in range(EPG):
        s = jnp.logical_and(jnp.logical_and(rest[j] == top2, jnp.logical_not(is1[j])),
                            jnp.logical_not(taken))
        is2.append(s)
        taken = jnp.logical_or(taken, s)
    w1 = 1.0 / (1.0 + jnp.exp(top2 - top1))
    w2 = 1.0 - w1
    comb = [jnp.where(is1[j], w1, jnp.where(is2[j], w2, 0.0)) * gw for j in range(EPG)]
    return sel, comb


def _moe_sort_kernel(h_ref, g_ref, wr2_ref, rb_ref, tri_ref, xs_ref, cs_ref, info_ref, seg_ref):
    vn = _rms_norm(h_ref[...], g_ref[...])
    v_hi, v_lo = _split_bf16(vn)
    wr2 = wr2_ref[...]
    r2 = _dot(v_hi, wr2)
    logits = r2[:, :LANES] + r2[:, LANES:] + _dot(v_lo, wr2[:, :LANES]) + rb_ref[...]
    sel, comb = _route(logits.T)

    onehot = jnp.concatenate([s.astype(F32) for s in sel] + [jnp.zeros((8 - N_GROUPS, TS), F32)], axis=0)
    cnt = _dot(onehot.astype(BF16), tri_ref[...])
    totals = [jnp.max(cnt[g:g + 1, :], axis=1, keepdims=True) for g in range(N_GROUPS)]
    starts = [jnp.zeros((1, 1), F32)]
    for g in range(N_GROUPS - 1):
        starts.append(starts[g] + totals[g])
    slot = jnp.zeros((1, TS), F32)
    for g in range(N_GROUPS):
        slot = jnp.where(sel[g], starts[g] + cnt[g:g + 1, :] - 1.0, slot)

    info = jnp.concatenate(comb + [slot, jnp.zeros((LANES - EPG - 1, TS), F32)], axis=0).T
    info_ref[...] = info

    srow = lax.broadcasted_iota(jnp.int32, (TS, TS), 0)
    perm = jnp.where(srow == slot.astype(jnp.int32), 1.0, 0.0).astype(BF16)
    lane = lax.broadcasted_iota(jnp.int32, (TS, LANES), 1)
    c_hi = info.astype(BF16).astype(F32)
    c_lo = (info - c_hi).astype(BF16).astype(F32)
    pack = jnp.where(lane < EPG, c_hi, pltpu.roll(c_lo, EPG, 1)).astype(BF16)
    xs_ext = _dot(perm, jnp.concatenate([v_hi, pack], axis=1))
    xs_ref[...] = xs_ext[:, :D].astype(BF16)
    cs = xs_ext[:, D:]
    cs_ref[...] = cs + pltpu.roll(cs, LANES - EPG, 1)

    lane1 = lax.broadcasted_iota(jnp.int32, (1, LANES), 1)
    seg = jnp.zeros((1, LANES), F32)
    for g in range(N_GROUPS):
        seg = jnp.where(lane1 == g, starts[g], seg)
        seg = jnp.where(lane1 == N_GROUPS + g, starts[g] + totals[g], seg)
    seg_ref[0] = seg.astype(jnp.int32)


def _moe_sort(h, g, wr2, rb, tri):
    tp = h.shape[0]
    nblk = tp // TS
    const = lambda x: pl.BlockSpec(x.shape, lambda i: (0, 0))
    row = lambda w: pl.BlockSpec((TS, w), lambda i: (i, 0))
    return pl.pallas_call(
        _moe_sort_kernel,
        grid=(nblk,),
        in_specs=[row(D), const(g), const(wr2), const(rb), const(tri)],
        out_specs=[row(D), row(LANES), row(LANES), pl.BlockSpec((1, 1, LANES), lambda i: (i, 0, 0))],
        out_shape=[jax.ShapeDtypeStruct((tp, D), BF16), jax.ShapeDtypeStruct((tp, LANES), F32),
                   jax.ShapeDtypeStruct((tp, LANES), F32), jax.ShapeDtypeStruct((nblk, 1, LANES), jnp.int32)],
        compiler_params=pltpu.CompilerParams(dimension_semantics=("arbitrary",),
                                             vmem_limit_bytes=VMEM_LIMIT),
        name="moe_sort",
    )(h, g, wr2, rb, tri)


def _moe_ffn_kernel(seg_ref, xs_ref, cs_ref, info_ref, h_ref, w1_ref, w3_ref, w2_ref, out_ref, ys_ref):
    i = pl.program_id(0)
    ys_ref[...] = jnp.zeros_like(ys_ref)

    def group(g, carry):
        start = seg_ref[i * LANES + g]
        end = seg_ref[i * LANES + N_GROUPS + g]
        first = (start // SEG_ALIGN) * SEG_ALIGN
        n_win = jnp.where(end > start, (end - first + SEG - 1) // SEG, 0)

        def window(w, c2):
            lo = first + w * SEG
            ws = pl.multiple_of(jnp.minimum(lo, TS - SEG), SEG_ALIGN)
            rows = pl.ds(ws, SEG)
            x = xs_ref[rows, :]
            h1 = _dot(x, w1_ref[g])
            h3 = _dot(x, w3_ref[g])
            he = h1 * jax.nn.sigmoid(h1) * h3
            r = ws + lax.broadcasted_iota(jnp.int32, (SEG, 1), 0)
            inseg = jnp.logical_and(r >= jnp.maximum(start, lo), r < jnp.minimum(end, lo + SEG))
            cs = cs_ref[rows, :]
            parts = []
            for j in range(EPG):
                wcol = jnp.where(inseg, cs[:, j:j + 1], 0.0)
                parts.append((he[:, j * EXPERT_HIDDEN:(j + 1) * EXPERT_HIDDEN] * wcol).astype(BF16))
            ys_ref[rows, :] += _dot(jnp.concatenate(parts, axis=1), w2_ref[g])
            return c2

        lax.fori_loop(0, n_win, window, 0)
        return carry

    lax.fori_loop(0, N_GROUPS, group, 0)

    slot = info_ref[:, SLOT_LANE:SLOT_LANE + 1].astype(jnp.int32)
    scol = lax.broadcasted_iota(jnp.int32, (TS, TS), 1)
    unperm = jnp.where(scol == slot, 1.0, 0.0).astype(BF16)
    out_ref[...] = h_ref[...] + _dot(unperm, ys_ref[...].astype(BF16))


def _moe_ffn(seg, xs, cs, info, h, w1g, w3g, w2g):
    tp = h.shape[0]
    nblk = tp // TS
    row = lambda w: pl.BlockSpec((TS, w), lambda i, s: (i, 0))
    resident = lambda x: pl.BlockSpec(x.shape, lambda i, s: (0, 0, 0), pipeline_mode=pl.Buffered(1))
    return pl.pallas_call(
        _moe_ffn_kernel,
        grid_spec=pltpu.PrefetchScalarGridSpec(
            num_scalar_prefetch=1,
            grid=(nblk,),
            in_specs=[row(D), row(LANES), row(LANES), row(D), resident(w1g), resident(w3g), resident(w2g)],
            out_specs=row(D),
            scratch_shapes=[pltpu.VMEM((TS, D), F32)]),
        out_shape=jax.ShapeDtypeStruct((tp, D), F32),
        compiler_params=pltpu.CompilerParams(dimension_semantics=("arbitrary",),
                                             vmem_limit_bytes=MOE_VMEM_LIMIT),
        name="moe_ffn",
    )(seg, xs, cs, info, h, w1g, w3g, w2g)


def _final_norm_kernel(h_ref, g_ref, out_ref):
    out_ref[...] = _rms_norm(h_ref[...], g_ref[...])


def _final_norm(h, g, batch, seq_out):
    nblk = seq_out // TM
    return pl.pallas_call(
        _final_norm_kernel,
        grid=(batch, nblk),
        in_specs=[pl.BlockSpec((pl.Element(TM), pl.Element(D)),
                               lambda b, r: (pl.multiple_of(b * LP + N_META + r * TM, N_META), 0)),
                  pl.BlockSpec((1, D), lambda b, r: (0, 0))],
        out_specs=pl.BlockSpec((TM, D), lambda b, r: (b * nblk + r, 0)),
        out_shape=jax.ShapeDtypeStruct((batch * seq_out, D), F32),
        compiler_params=pltpu.CompilerParams(dimension_semantics=("arbitrary", "arbitrary")),
        name="final_norm",
    )(h, g)


def _pad_lanes(x, n=LANES):
    return jnp.pad(x, ((0, 0), (0, n - x.shape[1])))


def _pad_rows(w, n):
    return jnp.pad(w, ((0, n - w.shape[0]), (0, 0)))


def _hi_lo_weight(w):
    hi = w.astype(BF16)
    lo = (w - hi.astype(F32)).astype(BF16)
    return jnp.concatenate([_pad_lanes(hi), _pad_lanes(lo)], axis=1)


def _per_head_cols(w):
    k = w.shape[0]
    w = w.reshape(k, HEADS, HEAD_DIM)
    return jnp.pad(w, ((0, 0), (0, 0), (0, HEAD_PAD - HEAD_DIM))).reshape(k, ATT_W)


def _bias_select():
    sel = jnp.zeros((LANES, 2, HEADS, HEAD_PAD), F32)
    one_lane = N_FPIECE * HEADS
    for p in range(N_FPIECE):
        for hd in range(HEADS):
            sel = sel.at[p * HEADS + hd, 0, hd, HEAD_DIM + p].set(1.0)
            sel = sel.at[p * HEADS + hd, 1, hd, HEAD_DIM + N_FPIECE + p].set(-1.0)
    sel = sel.at[one_lane, 0, :, HEAD_DIM + N_FPIECE:HEAD_DIM + 2 * N_FPIECE].set(1.0)
    sel = sel.at[one_lane, 1, :, HEAD_DIM:HEAD_DIM + N_FPIECE].set(1.0)
    return sel.reshape(LANES, 2 * ATT_W).astype(BF16)


def kernel(x, meta, norm1_g, w_in, b_forget, pool_w, pool_b, pool_scale, conv_w, conv_b, conv_ln_g,
           conv_ln_b, w_out_a, w_out_b, w_out_c, w_o, norm2_g, router_g, router_g_b, router_e,
           router_e_b, exp_w1, exp_w3, exp_w2, final_g):
    B, L, _ = x.shape
    depth = w_in.shape[0]
    assert LP == L + TM and LP % TQ == 0 and meta.shape[0] == N_META
    assert (B * LP) % TS == 0 and (B * LP) % TM_OUT == 0
    h = (x.reshape(B * L, D), meta.astype(x.dtype))
    tri = jnp.tril(jnp.ones((TM, TM), BF16))
    tri_u = jnp.triu(jnp.ones((TS, TS), BF16))
    sel = _bias_select()

    c_abc = POOL_W + 2 * CONV_W
    c_q = c_abc + FOX_W
    c_k = c_q + FOX_W
    c_v = c_k + FOX_W
    c_f = c_v + HEADS
    for l in range(depth):
        wl = w_in[l]
        wabc = wl[:, :c_abc].astype(BF16)
        wqk = jnp.concatenate([wl[:, c_abc:c_q] * (HEAD_DIM ** -0.5 * LOG2E), wl[:, c_q:c_k]],
                              axis=1).astype(BF16)
        wvt = _per_head_cols(wl[:, c_k:c_v]).T.astype(BF16)
        wf = _pad_lanes(wl[:, c_v:c_f]).astype(BF16)
        wg = wl[:, c_f:].astype(BF16)
        bf = _pad_lanes(b_forget[l][None, :])
        pw = jax.scipy.linalg.block_diag(*[pool_w[l, g] for g in range(pool_w.shape[1])]).astype(BF16)
        pool_conv = (pw, pool_b[l].reshape(1, POOL_W), pool_scale[l][None, :], _pad_rows(conv_w[l], 32),
                     conv_b[l][None, :], conv_ln_g[l][None, :], conv_ln_b[l][None, :])
        ya, yb, q, k, vt, *h0 = _in_proj(h, norm1_g[l][None, :], wabc, wqk, wvt, wf, bf, tri, sel, pool_conv, B)
        if h0:
            h = h0[0]
        o = _attention(q, k, vt, B)
        h = _mixer_out(h, norm1_g[l][None, :], ya, yb, o, wg, w_out_a[l].astype(BF16),
                       w_out_b[l].astype(BF16), w_out_c[l].astype(BF16), w_o[l].astype(BF16))

        wr2 = _hi_lo_weight(jnp.concatenate([router_g[l], router_e[l]], axis=1))
        rb = _pad_lanes(jnp.concatenate([router_g_b[l], router_e_b[l]])[None, :])
        grouped = lambda w: (w.reshape(N_GROUPS, EPG, D, EXPERT_HIDDEN).transpose(0, 2, 1, 3)
                             .reshape(N_GROUPS, D, GROUP_HIDDEN).astype(BF16))
        xs, cs, info, seg = _moe_sort(h, norm2_g[l][None, :], wr2, rb, tri_u)
        h = _moe_ffn(seg.reshape(-1), xs, cs, info, h, grouped(exp_w1[l]), grouped(exp_w3[l]),
                     exp_w2[l].reshape(N_GROUPS, GROUP_HIDDEN, D).astype(BF16))

    assert L % TM == 0
    return _final_norm(h, final_g[None, :], B, L).reshape(B, L, D)
```

```python
import functools
import math

import jax
import jax.numpy as jnp
from jax import lax
from jax.experimental import pallas as pl
from jax.experimental.pallas import tpu as pltpu

D = 1024
N_META = 16
EPS = 1e-6
POOL_W = 256
POOL_WINDOWS = (2, 4, 8, 16)
CONV_W = 256
CONV_K = 31
HEADS = 8
HEAD_DIM = 64
FOX_W = HEADS * HEAD_DIM
N_GROUPS = 4
EPG = 4
N_EXPERTS = 16
EXPERT_HIDDEN = 256
GROUP_HIDDEN = EPG * EXPERT_HIDDEN

LANES = 128
HEAD_PAD = LANES
ATT_W = HEADS * HEAD_PAD
LP = 8704
TM = 512
BLOCKS_PER_SEQ = LP // TM
HALO = 32
TQ = 512
N_FPIECE = 3
VMEM_LIMIT = 56 * 1024 * 1024
MOE_VMEM_LIMIT = 60 * 1024 * 1024
LOG2E = math.log2(math.e)

F32 = jnp.float32
BF16 = jnp.bfloat16


def _dot(a, b):
    return jnp.dot(a, b, preferred_element_type=F32)


def _dot_nt(a, b):
    return lax.dot_general(a, b, (((1,), (1,)), ((), ())), preferred_element_type=F32)


def _rms_norm(x, g):
    ms = jnp.mean(x * x, axis=-1, keepdims=True)
    return x * lax.rsqrt(ms + EPS) * g


def _split_bf16(x):
    hi = x.astype(BF16)
    lo = (x - hi.astype(F32)).astype(BF16)
    return hi, lo


CONV_CHUNK = 64
PH_ROWS = TM + HALO - 8


def _in_proj_kernel(first_layer, *refs):
    if first_layer:
        x_ref, meta_ref, *refs = refs
    else:
        h_ref, *refs = refs
    (g_ref, wabc_ref, wqk_ref, wvt_ref, wf_ref, bf_ref, tri_ref, sel_ref,
     pw_ref, pb_ref, ps_ref, cw_ref, cb_ref, lg_ref, lb_ref,
     ya_ref, yb_ref, q_ref, k_ref, vt_ref, *refs) = refs
    if first_layer:
        h_ref, *refs = refs
    carry_ref, xa_ref, xg_ref, ph_ref = refs
    i = pl.program_id(0)
    blk = i % BLOCKS_PER_SEQ

    if first_layer:
        @pl.when(blk == 0)
        def _():
            h_ref[0:N_META, :] = meta_ref[...]
            h_ref[N_META:, :] = x_ref[0:TM - N_META, :]

        @pl.when(blk == BLOCKS_PER_SEQ - 1)
        def _():
            h_ref[0:N_META, :] = x_ref[TM - N_META:, :]
            h_ref[N_META:, :] = jnp.zeros((TM - N_META, D), F32)

        @pl.when(jnp.logical_and(blk > 0, blk < BLOCKS_PER_SEQ - 1))
        def _():
            h_ref[...] = x_ref[...]

    @pl.when(blk == 0)
    def _():
        carry_ref[...] = jnp.zeros_like(carry_ref)
        xa_ref[0:HALO, :] = jnp.zeros((HALO, POOL_W), F32)
        xg_ref[0:HALO, :] = jnp.zeros((HALO, CONV_W), F32)

    u = _rms_norm(h_ref[...], g_ref[...]).astype(BF16)

    abc = _dot(u, wabc_ref[...])
    xa_ref[HALO:, :] = abc[:, :POOL_W]
    xg_ref[HALO:, :] = abc[:, POOL_W:POOL_W + CONV_W] * jax.nn.sigmoid(abc[:, POOL_W + CONV_W:])

    pos = blk * TM + lax.broadcasted_iota(jnp.int32, (TM, LANES), 0) + 1
    lane = lax.broadcasted_iota(jnp.int32, (TM, LANES), 1)
    low = lane < (LANES // 2)
    means = []
    for t in range(2):
        x = xa_ref[:, t * LANES:(t + 1) * LANES]
        s2 = x + pltpu.roll(x, 1, 0)
        s4 = s2 + pltpu.roll(s2, 2, 0)
        if t == 0:
            small, big = s2, s4
        else:
            s8 = s4 + pltpu.roll(s4, 4, 0)
            small, big = s8, s8 + pltpu.roll(s8, 8, 0)
        w_small, w_big = POOL_WINDOWS[2 * t], POOL_WINDOWS[2 * t + 1]
        cnt = jnp.minimum(pos, jnp.where(low, w_small, w_big)).astype(F32)
        means.append(jnp.where(low, small[HALO:], big[HALO:]) / cnt - x[HALO:])
    diff = jnp.concatenate(means, axis=1).astype(BF16)
    ya_ref[...] = ((_dot(diff, pw_ref[...]) + pb_ref[...]) * ps_ref[...]).astype(BF16)

    for r in range(1, 8):
        ph_ref[r - 1] = xg_ref[r:r + PH_ROWS, :]
    cw = cw_ref[...]
    base = HALO - (CONV_K - 1)

    def conv_rows(c0):
        acc = jnp.zeros((CONV_CHUNK, CONV_W), F32) + cb_ref[...]
        for j in range(CONV_K):
            off = base + j
            r = off % 8
            rows = pl.ds(c0 + (off - r), CONV_CHUNK)
            tap = xg_ref[rows, :] if r == 0 else ph_ref[r - 1, rows, :]
            acc = acc + tap * cw[j:j + 1, :]
        mu = jnp.mean(acc, axis=-1, keepdims=True)
        cen = acc - mu
        var = jnp.mean(cen * cen, axis=-1, keepdims=True)
        y = cen * lax.rsqrt(var + EPS) * lg_ref[...] + lb_ref[...]
        yb_ref[c0:c0 + CONV_CHUNK, :] = (y * jax.nn.sigmoid(y)).astype(BF16)

    fg = _dot(u, wf_ref[...]) + bf_ref[...]
    logf = jnp.minimum(fg, 0.0) - jnp.log1p(jnp.exp(-jnp.abs(fg)))
    l_hi, l_lo = _split_bf16(logf)
    cum2 = _dot(tri_ref[...], jnp.concatenate([l_hi, l_lo], axis=1))
    cum = cum2[:, :LANES] + cum2[:, LANES:] + carry_ref[...]
    carry_ref[...] = cum[TM - 1:TM, :]

    lane = lax.broadcasted_iota(jnp.int32, (TM, LANES), 1)
    fv = jnp.where(lane < HEADS, cum * LOG2E, 0.0)
    f_hi = fv.astype(BF16).astype(F32)
    r1 = fv - f_hi
    f_mid = r1.astype(BF16).astype(F32)
    f_lo = (r1 - f_mid).astype(BF16).astype(F32)
    fcat = f_hi + pltpu.roll(f_mid, HEADS, 1) + pltpu.roll(f_lo, 2 * HEADS, 1)
    fcat = jnp.where(lane == N_FPIECE * HEADS, 1.0, fcat).astype(BF16)

    for c0 in range(0, TM, CONV_CHUNK):
        conv_rows(c0)

    qkc = _dot(u, wqk_ref[...])
    bias_cols = _dot(fcat, sel_ref[...])
    from_proj = lane < HEAD_DIM
    for part, out_ref in enumerate((q_ref, k_ref)):
        for hd in range(HEADS):
            src = qkc[:, part * FOX_W + (hd // 2) * LANES:part * FOX_W + (hd // 2 + 1) * LANES]
            if hd % 2 == 1:
                src = pltpu.roll(src, HEAD_DIM, 1)
            tile = slice(hd * HEAD_PAD, (hd + 1) * HEAD_PAD)
            bias = bias_cols[:, part * ATT_W + hd * HEAD_PAD:part * ATT_W + (hd + 1) * HEAD_PAD]
            out_ref[:, tile] = jnp.where(from_proj, src, bias).astype(BF16)

    xa_ref[0:HALO, :] = xa_ref[TM:TM + HALO, :]
    xg_ref[0:HALO, :] = xg_ref[TM:TM + HALO, :]

    vt = _dot_nt(wvt_ref[...], u)
    row = lax.broadcasted_iota(jnp.int32, (ATT_W, TM), 0)
    vt_ref[0] = jnp.where(row % HEAD_PAD == HEAD_DIM, 1.0, vt).astype(BF16)


def _in_proj(src, g, wabc, wqk, wvt, wf, bf, tri, sel, pool_conv, batch):
    first_layer = isinstance(src, tuple)
    tp = batch * LP
    nblk = tp // TM
    row = lambda w: pl.BlockSpec((TM, w), lambda i: (i, 0))
    const = lambda x: pl.BlockSpec(x.shape, lambda i: (0, 0))
    if first_layer:
        x, meta = src
        seq_x = x.shape[0] // batch

        def x_rows(i):
            start = jnp.clip((i % BLOCKS_PER_SEQ) * TM - N_META, 0, seq_x - TM)
            return pl.multiple_of((i // BLOCKS_PER_SEQ) * seq_x + start, N_META), 0

        src_args = [x, meta]
        src_specs = [pl.BlockSpec((pl.Element(TM), pl.Element(D)), x_rows), const(meta)]
        h_spec, h_shape = [row(D)], [jax.ShapeDtypeStruct((tp, D), F32)]
    else:
        src_args, src_specs, h_spec, h_shape = [src], [row(D)], [], []
    return pl.pallas_call(
        functools.partial(_in_proj_kernel, first_layer),
        grid=(nblk,),
        in_specs=src_specs + [const(g), const(wabc), const(wqk), const(wvt), const(wf), const(bf),
                              const(tri), const(sel)] + [const(p) for p in pool_conv],
        out_specs=[row(POOL_W), row(CONV_W), row(ATT_W), row(ATT_W),
                   pl.BlockSpec((1, ATT_W, TM), lambda i: (i // BLOCKS_PER_SEQ, 0, i % BLOCKS_PER_SEQ))] + h_spec,
        out_shape=[jax.ShapeDtypeStruct((tp, POOL_W), BF16), jax.ShapeDtypeStruct((tp, CONV_W), BF16),
                   jax.ShapeDtypeStruct((tp, ATT_W), BF16), jax.ShapeDtypeStruct((tp, ATT_W), BF16),
                   jax.ShapeDtypeStruct((batch, ATT_W, LP), BF16)] + h_shape,
        scratch_shapes=[pltpu.VMEM((1, LANES), F32),
                        pltpu.VMEM((TM + HALO, POOL_W), F32), pltpu.VMEM((TM + HALO, CONV_W), F32),
                        pltpu.VMEM((7, PH_ROWS, CONV_W), F32)],
        compiler_params=pltpu.CompilerParams(dimension_semantics=("arbitrary",),
                                             vmem_limit_bytes=VMEM_LIMIT),
        name="in_proj",
    )(*src_args, g, wabc, wqk, wvt, wf, bf, tri, sel, *pool_conv)


HEADS_PER_STEP = 4


def _attn_kernel(q_ref, k_ref, vt_ref, o_ref, m_ref, acc_ref, s_ref, mt_ref):
    qi = pl.program_id(2)
    m_ref[...] = jnp.full_like(m_ref, -jnp.inf)
    acc_ref[...] = jnp.zeros_like(acc_ref)

    def scores(j, slot, masked=False):
        start = pl.multiple_of(j * TQ, TQ)
        for hd in range(HEADS_PER_STEP):
            cols = slice(hd * HEAD_PAD, (hd + 1) * HEAD_PAD)
            s = _dot_nt(k_ref[pl.ds(start, TQ), cols], q_ref[:, cols])
            if masked:
                key = lax.broadcasted_iota(jnp.int32, (TQ, TQ), 0)
                qry = lax.broadcasted_iota(jnp.int32, (TQ, TQ), 1)
                s = jnp.where(key <= qry, s, -jnp.inf)
            s_ref[slot, hd] = s
            mt_ref[slot, hd] = jnp.max(s, axis=0, keepdims=True)

    def values(j, slot):
        start = pl.multiple_of(j * TQ, TQ)
        for hd in range(HEADS_PER_STEP):
            cols = slice(hd * HEAD_PAD, (hd + 1) * HEAD_PAD)
            m_old = m_ref[hd]
            m_new = jnp.maximum(m_old, mt_ref[slot, hd])
            p = jnp.exp2(s_ref[slot, hd] - m_new).astype(BF16)
            alpha = jnp.exp2(m_old - m_new)
            pv = _dot(vt_ref[0, cols, pl.ds(start, TQ)], p)
            acc_ref[hd] = alpha * acc_ref[hd] + pv
            m_ref[hd] = m_new

    @pl.when(qi == 0)
    def _():
        scores(0, 0, masked=True)
        values(0, 0)

    @pl.when(qi > 0)
    def _():
        scores(0, 0)

        def two_tiles(j):
            scores(j + 1, 1)
            values(j, 0)
            scores(j + 2, 0)
            values(j + 1, 1)

        def body(t, c):
            two_tiles(4 * t)
            two_tiles(4 * t + 2)
            return c

        last = qi - 1
        n_pairs = last // 2
        lax.fori_loop(0, n_pairs // 2, body, 0)

        @pl.when(n_pairs % 2 == 1)
        def _():
            two_tiles(2 * (n_pairs - 1))

        @pl.when(last % 2 == 0)
        def _():
            scores(qi, 1, masked=True)
            values(last, 0)
            values(qi, 1)

        @pl.when(last % 2 == 1)
        def _():
            scores(last, 1)
            values(last - 1, 0)
            scores(qi, 0, masked=True)
            values(last, 1)
            values(qi, 0)

    outs = []
    for hd in range(HEADS_PER_STEP):
        acc = acc_ref[hd]
        outs.append(acc[:HEAD_DIM, :] / acc[HEAD_DIM:HEAD_DIM + 1, :])
    o_ref[...] = jnp.concatenate(outs, axis=0).T.astype(BF16)


def _attention(q, k, vt, batch):
    nq = LP // TQ
    w = HEADS_PER_STEP * HEAD_PAD
    return pl.pallas_call(
        _attn_kernel,
        grid=(batch, HEADS // HEADS_PER_STEP, nq),
        in_specs=[pl.BlockSpec((TQ, w), lambda b, h, i: (b * nq + i, h)),
                  pl.BlockSpec((LP, w), lambda b, h, i: (b, h)),
                  pl.BlockSpec((1, w, LP), lambda b, h, i: (b, h, 0))],
        out_specs=pl.BlockSpec((TQ, HEADS_PER_STEP * HEAD_DIM), lambda b, h, i: (b * nq + i, h)),
        out_shape=jax.ShapeDtypeStruct((batch * LP, FOX_W), BF16),
        scratch_shapes=[pltpu.VMEM((HEADS_PER_STEP, 1, TQ), F32),
                        pltpu.VMEM((HEADS_PER_STEP, HEAD_PAD, TQ), F32),
                        pltpu.VMEM((2, HEADS_PER_STEP, TQ, TQ), F32),
                        pltpu.VMEM((2, HEADS_PER_STEP, 1, TQ), F32)],
        compiler_params=pltpu.CompilerParams(dimension_semantics=("arbitrary", "arbitrary", "arbitrary"),
                                             vmem_limit_bytes=VMEM_LIMIT),
        name="fox_attention",
    )(q, k, vt)


TM_OUT = 1024


def _mixer_out_kernel(h_ref, g_ref, ya_ref, yb_ref, o_ref, wg_ref,
                      woa_ref, wob_ref, woc_ref, wo_ref, out_ref):
    h = h_ref[...]
    u = _rms_norm(h, g_ref[...]).astype(BF16)
    branches = (_dot(ya_ref[...], woa_ref[...]), _dot(yb_ref[...], wob_ref[...]),
                _dot(o_ref[...], woc_ref[...]))
    merged = None
    for n, y in enumerate(branches):
        gated = jax.nn.sigmoid(_dot(u, wg_ref[:, n * D:(n + 1) * D])) * y
        merged = gated if merged is None else merged + gated
    out_ref[...] = h + _dot(merged.astype(BF16), wo_ref[...])


def _mixer_out(h, g, ya, yb, o, wg, woa, wob, woc, wo):
    tp = h.shape[0]
    nblk = tp // TM_OUT
    row = lambda w: pl.BlockSpec((TM_OUT, w), lambda i: (i, 0))
    const = lambda x: pl.BlockSpec(x.shape, lambda i: (0, 0), pipeline_mode=pl.Buffered(1))
    return pl.pallas_call(
        _mixer_out_kernel,
        grid=(nblk,),
        in_specs=[row(D), const(g), row(POOL_W), row(CONV_W), row(FOX_W),
                  const(wg), const(woa), const(wob), const(woc), const(wo)],
        out_specs=row(D),
        out_shape=jax.ShapeDtypeStruct((tp, D), F32),
        compiler_params=pltpu.CompilerParams(dimension_semantics=("arbitrary",),
                                             vmem_limit_bytes=VMEM_LIMIT),
        name="mixer_out",
    )(h, g, ya, yb, o, wg, woa, wob, woc, wo)


TS = 1024
SEG_SIZES = (256, 320, 384, 448, 512)
SEG_ALIGN = 64
SLOT_LANE = EPG


def _route(lt):
    gl = [lt[g:g + 1, :] for g in range(N_GROUPS)]
    gmax = functools.reduce(jnp.maximum, gl)
    gsum = functools.reduce(lambda x, y: x + y, [jnp.exp(x - gmax) for x in gl])
    gw = 1.0 / gsum
    sel, taken = [], jnp.zeros_like(gmax, dtype=jnp.bool_)
    for g in range(N_GROUPS):
        s = jnp.logical_and(gl[g] == gmax, jnp.logical_not(taken))
        sel.append(s)
        taken = jnp.logical_or(taken, s)
    el = []
    for j in range(EPG):
        e = jnp.zeros_like(gmax)
        for g in range(N_GROUPS):
            e = jnp.where(sel[g], lt[N_GROUPS + g * EPG + j:N_GROUPS + g * EPG + j + 1, :], e)
        el.append(e)
    neg = jnp.full_like(gmax, -jnp.inf)
    top1 = functools.reduce(jnp.maximum, el)
    is1, taken = [], jnp.zeros_like(gmax, dtype=jnp.bool_)
    for j in range(EPG):
        s = jnp.logical_and(el[j] == top1, jnp.logical_not(taken))
        is1.append(s)
        taken = jnp.logical_or(taken, s)
    rest = [jnp.where(is1[j], neg, el[j]) for j in range(EPG)]
    top2 = functools.reduce(jnp.maximum, rest)
    is2, taken = [], jnp.zeros_like(gmax, dtype=jnp.bool_)
    for j in range(EPG):
        s = jnp.logical_and(jnp.logical_and(rest[j] == top2, jnp.logical_not(is1[j])),
                            jnp.logical_not(taken))
        is2.append(s)
        taken = jnp.logical_or(taken, s)
    w1 = 1.0 / (1.0 + jnp.exp(top2 - top1))
    w2 = 1.0 - w1
    comb = [jnp.where(is1[j], w1, jnp.where(is2[j], w2, 0.0)) * gw for j in range(EPG)]
    return sel, comb


def _moe_sort_kernel(h_ref, g_ref, wr2_ref, rb_ref, tri_ref, xs_ref, cs_ref, info_ref, seg_ref):
    vn = _rms_norm(h_ref[...], g_ref[...])
    v_hi, v_lo = _split_bf16(vn)
    wr2 = wr2_ref[...]
    r2 = _dot(v_hi, wr2)
    logits = r2[:, :LANES] + r2[:, LANES:] + _dot(v_lo, wr2[:, :LANES]) + rb_ref[...]
    sel, comb = _route(logits.T)

    onehot = jnp.concatenate([s.astype(F32) for s in sel] + [jnp.zeros((8 - N_GROUPS, TS), F32)], axis=0)
    cnt = _dot(onehot.astype(BF16), tri_ref[...])
    totals = [jnp.max(cnt[g:g + 1, :], axis=1, keepdims=True) for g in range(N_GROUPS)]
    starts = [jnp.zeros((1, 1), F32)]
    for g in range(N_GROUPS - 1):
        starts.append(starts[g] + totals[g])
    slot = jnp.zeros((1, TS), F32)
    for g in range(N_GROUPS):
        slot = jnp.where(sel[g], starts[g] + cnt[g:g + 1, :] - 1.0, slot)

    info = jnp.concatenate(comb + [slot, jnp.zeros((LANES - EPG - 1, TS), F32)], axis=0).T
    info_ref[...] = info

    srow = lax.broadcasted_iota(jnp.int32, (TS, TS), 0)
    perm = jnp.where(srow == slot.astype(jnp.int32), 1.0, 0.0).astype(BF16)
    lane = lax.broadcasted_iota(jnp.int32, (TS, LANES), 1)
    c_hi = info.astype(BF16).astype(F32)
    c_lo = (info - c_hi).astype(BF16).astype(F32)
    pack = jnp.where(lane < EPG, c_hi, pltpu.roll(c_lo, EPG, 1)).astype(BF16)
    xs_ext = _dot(perm, jnp.concatenate([v_hi, pack], axis=1))
    xs_ref[...] = xs_ext[:, :D].astype(BF16)
    cs = xs_ext[:, D:]
    cs_ref[...] = cs + pltpu.roll(cs, LANES - EPG, 1)

    lane1 = lax.broadcasted_iota(jnp.int32, (1, LANES), 1)
    seg = jnp.zeros((1, LANES), F32)
    for g in range(N_GROUPS):
        seg = jnp.where(lane1 == g, starts[g], seg)
        seg = jnp.where(lane1 == N_GROUPS + g, starts[g] + totals[g], seg)
    seg_ref[0] = seg.astype(jnp.int32)


def _moe_sort(h, g, wr2, rb, tri):
    tp = h.shape[0]
    nblk = tp // TS
    const = lambda x: pl.BlockSpec(x.shape, lambda i: (0, 0))
    row = lambda w: pl.BlockSpec((TS, w), lambda i: (i, 0))
    return pl.pallas_call(
        _moe_sort_kernel,
        grid=(nblk,),
        in_specs=[row(D), const(g), const(wr2), const(rb), const(tri)],
        out_specs=[row(D), row(LANES), row(LANES), pl.BlockSpec((1, 1, LANES), lambda i: (i, 0, 0))],
        out_shape=[jax.ShapeDtypeStruct((tp, D), BF16), jax.ShapeDtypeStruct((tp, LANES), F32),
                   jax.ShapeDtypeStruct((tp, LANES), F32), jax.ShapeDtypeStruct((nblk, 1, LANES), jnp.int32)],
        compiler_params=pltpu.CompilerParams(dimension_semantics=("arbitrary",),
                                             vmem_limit_bytes=VMEM_LIMIT),
        name="moe_sort",
    )(h, g, wr2, rb, tri)


def _moe_ffn_kernel(seg_ref, xs_ref, cs_ref, info_ref, h_ref, w1_ref, w3_ref, w2_ref, out_ref, ys_ref):
    i = pl.program_id(0)
    ys_ref[...] = jnp.zeros_like(ys_ref)

    def window(g, start, end, lo, size):
        ws = pl.multiple_of(jnp.minimum(lo, TS - size), SEG_ALIGN)
        rows = pl.ds(ws, size)
        x = xs_ref[rows, :]
        h1 = _dot(x, w1_ref[g])
        h3 = _dot(x, w3_ref[g])
        he = h1 * jax.nn.sigmoid(h1) * h3
        r = ws + lax.broadcasted_iota(jnp.int32, (size, 1), 0)
        inseg = jnp.logical_and(r >= jnp.maximum(start, lo), r < jnp.minimum(end, lo + size))
        cs = cs_ref[rows, :]
        parts = []
        for j in range(EPG):
            wcol = jnp.where(inseg, cs[:, j:j + 1], 0.0)
            parts.append((he[:, j * EXPERT_HIDDEN:(j + 1) * EXPERT_HIDDEN] * wcol).astype(BF16))
        ys_ref[rows, :] += _dot(jnp.concatenate(parts, axis=1), w2_ref[g])

    def group(g, carry):
        start = seg_ref[i * LANES + g]
        end = seg_ref[i * LANES + N_GROUPS + g]
        first = (start // SEG_ALIGN) * SEG_ALIGN
        need = jnp.where(end > start, end - first, 0)
        n_full = jnp.maximum(need - 1, 0) // SEG_SIZES[-1]

        def full(w, c2):
            window(g, start, end, first + w * SEG_SIZES[-1], SEG_SIZES[-1])
            return c2

        lax.fori_loop(0, n_full, full, 0)
        rest = need - n_full * SEG_SIZES[-1]
        for n, size in enumerate(SEG_SIZES):
            fits = jnp.logical_and(rest > (SEG_SIZES[n - 1] if n else 0), rest <= size)

            @pl.when(fits)
            def _():
                window(g, start, end, first + n_full * SEG_SIZES[-1], size)

        return carry

    lax.fori_loop(0, N_GROUPS, group, 0)

    slot = info_ref[:, SLOT_LANE:SLOT_LANE + 1].astype(jnp.int32)
    scol = lax.broadcasted_iota(jnp.int32, (TS, TS), 1)
    unperm = jnp.where(scol == slot, 1.0, 0.0).astype(BF16)
    out_ref[...] = h_ref[...] + _dot(unperm, ys_ref[...].astype(BF16))


def _moe_ffn(seg, xs, cs, info, h, w1g, w3g, w2g):
    tp = h.shape[0]
    nblk = tp // TS
    row = lambda w: pl.BlockSpec((TS, w), lambda i, s: (i, 0))
    resident = lambda x: pl.BlockSpec(x.shape, lambda i, s: (0, 0, 0), pipeline_mode=pl.Buffered(1))
    return pl.pallas_call(
        _moe_ffn_kernel,
        grid_spec=pltpu.PrefetchScalarGridSpec(
            num_scalar_prefetch=1,
            grid=(nblk,),
            in_specs=[row(D), row(LANES), row(LANES), row(D), resident(w1g), resident(w3g), resident(w2g)],
            out_specs=row(D),
            scratch_shapes=[pltpu.VMEM((TS, D), F32)]),
        out_shape=jax.ShapeDtypeStruct((tp, D), F32),
        compiler_params=pltpu.CompilerParams(dimension_semantics=("arbitrary",),
                                             vmem_limit_bytes=MOE_VMEM_LIMIT),
        name="moe_ffn",
    )(seg, xs, cs, info, h, w1g, w3g, w2g)


def _final_norm_kernel(h_ref, g_ref, out_ref):
    out_ref[...] = _rms_norm(h_ref[...], g_ref[...])


def _final_norm(h, g, batch, seq_out):
    nblk = seq_out // TM
    return pl.pallas_call(
        _final_norm_kernel,
        grid=(batch, nblk),
        in_specs=[pl.BlockSpec((pl.Element(TM), pl.Element(D)),
                               lambda b, r: (pl.multiple_of(b * LP + N_META + r * TM, N_META), 0)),
                  pl.BlockSpec((1, D), lambda b, r: (0, 0))],
        out_specs=pl.BlockSpec((TM, D), lambda b, r: (b * nblk + r, 0)),
        out_shape=jax.ShapeDtypeStruct((batch * seq_out, D), F32),
        compiler_params=pltpu.CompilerParams(dimension_semantics=("arbitrary", "arbitrary")),
        name="final_norm",
    )(h, g)


def _pad_lanes(x, n=LANES):
    return jnp.pad(x, ((0, 0), (0, n - x.shape[1])))


def _pad_rows(w, n):
    return jnp.pad(w, ((0, n - w.shape[0]), (0, 0)))


def _hi_lo_weight(w):
    hi = w.astype(BF16)
    lo = (w - hi.astype(F32)).astype(BF16)
    return jnp.concatenate([_pad_lanes(hi), _pad_lanes(lo)], axis=1)


def _per_head_cols(w):
    k = w.shape[0]
    w = w.reshape(k, HEADS, HEAD_DIM)
    return jnp.pad(w, ((0, 0), (0, 0), (0, HEAD_PAD - HEAD_DIM))).reshape(k, ATT_W)


def _bias_select():
    sel = jnp.zeros((LANES, 2, HEADS, HEAD_PAD), F32)
    one_lane = N_FPIECE * HEADS
    for p in range(N_FPIECE):
        for hd in range(HEADS):
            sel = sel.at[p * HEADS + hd, 0, hd, HEAD_DIM + p].set(1.0)
            sel = sel.at[p * HEADS + hd, 1, hd, HEAD_DIM + N_FPIECE + p].set(-1.0)
    sel = sel.at[one_lane, 0, :, HEAD_DIM + N_FPIECE:HEAD_DIM + 2 * N_FPIECE].set(1.0)
    sel = sel.at[one_lane, 1, :, HEAD_DIM:HEAD_DIM + N_FPIECE].set(1.0)
    return sel.reshape(LANES, 2 * ATT_W).astype(BF16)


def kernel(x, meta, norm1_g, w_in, b_forget, pool_w, pool_b, pool_scale, conv_w, conv_b, conv_ln_g,
           conv_ln_b, w_out_a, w_out_b, w_out_c, w_o, norm2_g, router_g, router_g_b, router_e,
           router_e_b, exp_w1, exp_w3, exp_w2, final_g):
    B, L, _ = x.shape
    depth = w_in.shape[0]
    assert LP == L + TM and LP % TQ == 0 and meta.shape[0] == N_META
    assert (B * LP) % TS == 0 and (B * LP) % TM_OUT == 0
    h = (x.reshape(B * L, D), meta.astype(x.dtype))
    tri = jnp.tril(jnp.ones((TM, TM), BF16))
    tri_u = jnp.triu(jnp.ones((TS, TS), BF16))
    sel = _bias_select()

    c_abc = POOL_W + 2 * CONV_W
    c_q = c_abc + FOX_W
    c_k = c_q + FOX_W
    c_v = c_k + FOX_W
    c_f = c_v + HEADS
    for l in range(depth):
        wl = w_in[l]
        wabc = wl[:, :c_abc].astype(BF16)
        wqk = jnp.concatenate([wl[:, c_abc:c_q] * (HEAD_DIM ** -0.5 * LOG2E), wl[:, c_q:c_k]],
                              axis=1).astype(BF16)
        wvt = _per_head_cols(wl[:, c_k:c_v]).T.astype(BF16)
        wf = _pad_lanes(wl[:, c_v:c_f]).astype(BF16)
        wg = wl[:, c_f:].astype(BF16)
        bf = _pad_lanes(b_forget[l][None, :])
        pw = jax.scipy.linalg.block_diag(*[pool_w[l, g] for g in range(pool_w.shape[1])]).astype(BF16)
        pool_conv = (pw, pool_b[l].reshape(1, POOL_W), pool_scale[l][None, :], _pad_rows(conv_w[l], 32),
                     conv_b[l][None, :], conv_ln_g[l][None, :], conv_ln_b[l][None, :])
        ya, yb, q, k, vt, *h0 = _in_proj(h, norm1_g[l][None, :], wabc, wqk, wvt, wf, bf, tri, sel, pool_conv, B)
        if h0:
            h = h0[0]
        o = _attention(q, k, vt, B)
        h = _mixer_out(h, norm1_g[l][None, :], ya, yb, o, wg, w_out_a[l].astype(BF16),
                       w_out_b[l].astype(BF16), w_out_c[l].astype(BF16), w_o[l].astype(BF16))

        wr2 = _hi_lo_weight(jnp.concatenate([router_g[l], router_e[l]], axis=1))
        rb = _pad_lanes(jnp.concatenate([router_g_b[l], router_e_b[l]])[None, :])
        grouped = lambda w: (w.reshape(N_GROUPS, EPG, D, EXPERT_HIDDEN).transpose(0, 2, 1, 3)
                             .reshape(N_GROUPS, D, GROUP_HIDDEN).astype(BF16))
        xs, cs, info, seg = _moe_sort(h, norm2_g[l][None, :], wr2, rb, tri_u)
        h = _moe_ffn(seg.reshape(-1), xs, cs, info, h, grouped(exp_w1[l]), grouped(exp_w3[l]),
                     exp_w2[l].reshape(N_GROUPS, GROUP_HIDDEN, D).astype(BF16))

    assert L % TM == 0
    return _final_norm(h, final_g[None, :], B, L).reshape(B, L, D)
```

```python
import functools
import math

import jax
import jax.numpy as jnp
from jax import lax
from jax.experimental import pallas as pl
from jax.experimental.pallas import tpu as pltpu

D = 1024
N_META = 16
EPS = 1e-6
POOL_W = 256
POOL_WINDOWS = (2, 4, 8, 16)
CONV_W = 256
CONV_K = 31
HEADS = 8
HEAD_DIM = 64
FOX_W = HEADS * HEAD_DIM
N_GROUPS = 4
EPG = 4
N_EXPERTS = 16
EXPERT_HIDDEN = 256
GROUP_HIDDEN = EPG * EXPERT_HIDDEN

LANES = 128
HEAD_PAD = LANES
ATT_W = HEADS * HEAD_PAD
LP = 8704
TM = 512
BLOCKS_PER_SEQ = LP // TM
HALO = 32
TQ = 512
N_FPIECE = 3
VMEM_LIMIT = 56 * 1024 * 1024
MOE_VMEM_LIMIT = 60 * 1024 * 1024
LOG2E = math.log2(math.e)

F32 = jnp.float32
BF16 = jnp.bfloat16


def _dot(a, b):
    return jnp.dot(a, b, preferred_element_type=F32)


def _dot_nt(a, b):
    return lax.dot_general(a, b, (((1,), (1,)), ((), ())), preferred_element_type=F32)


def _rms_norm(x, g):
    ms = jnp.mean(x * x, axis=-1, keepdims=True)
    return x * lax.rsqrt(ms + EPS) * g


def _split_bf16(x):
    hi = x.astype(BF16)
    lo = (x - hi.astype(F32)).astype(BF16)
    return hi, lo


CONV_CHUNK = 64
PH_ROWS = TM + HALO - 8


def _in_proj_kernel(first_layer, *refs):
    if first_layer:
        x_ref, meta_ref, *refs = refs
    else:
        h_ref, *refs = refs
    (g_ref, wabc_ref, wqk_ref, wvt_ref, wf_ref, bf_ref, tri_ref, sel_ref,
     pw_ref, pb_ref, ps_ref, cw_ref, cb_ref, lg_ref, lb_ref,
     ya_ref, yb_ref, q_ref, k_ref, vt_ref, *refs) = refs
    if first_layer:
        h_ref, *refs = refs
    carry_ref, xa_ref, xg_ref, ph_ref = refs
    i = pl.program_id(0)
    blk = i % BLOCKS_PER_SEQ

    if first_layer:
        @pl.when(blk == 0)
        def _():
            h_ref[0:N_META, :] = meta_ref[...]
            h_ref[N_META:, :] = x_ref[0:TM - N_META, :]

        @pl.when(blk == BLOCKS_PER_SEQ - 1)
        def _():
            h_ref[0:N_META, :] = x_ref[TM - N_META:, :]
            h_ref[N_META:, :] = jnp.zeros((TM - N_META, D), F32)

        @pl.when(jnp.logical_and(blk > 0, blk < BLOCKS_PER_SEQ - 1))
        def _():
            h_ref[...] = x_ref[...]

    @pl.when(blk == 0)
    def _():
        carry_ref[...] = jnp.zeros_like(carry_ref)
        xa_ref[0:HALO, :] = jnp.zeros((HALO, POOL_W), F32)
        xg_ref[0:HALO, :] = jnp.zeros((HALO, CONV_W), F32)

    u = _rms_norm(h_ref[...], g_ref[...]).astype(BF16)

    abc = _dot(u, wabc_ref[...])
    xa_ref[HALO:, :] = abc[:, :POOL_W]
    xg_ref[HALO:, :] = abc[:, POOL_W:POOL_W + CONV_W] * jax.nn.sigmoid(abc[:, POOL_W + CONV_W:])

    pos = blk * TM + lax.broadcasted_iota(jnp.int32, (TM, LANES), 0) + 1
    lane = lax.broadcasted_iota(jnp.int32, (TM, LANES), 1)
    low = lane < (LANES // 2)
    means = []
    for t in range(2):
        x = xa_ref[:, t * LANES:(t + 1) * LANES]
        s2 = x + pltpu.roll(x, 1, 0)
        s4 = s2 + pltpu.roll(s2, 2, 0)
        if t == 0:
            small, big = s2, s4
        else:
            s8 = s4 + pltpu.roll(s4, 4, 0)
            small, big = s8, s8 + pltpu.roll(s8, 8, 0)
        w_small, w_big = POOL_WINDOWS[2 * t], POOL_WINDOWS[2 * t + 1]
        cnt = jnp.minimum(pos, jnp.where(low, w_small, w_big)).astype(F32)
        means.append(jnp.where(low, small[HALO:], big[HALO:]) / cnt - x[HALO:])
    diff = jnp.concatenate(means, axis=1).astype(BF16)
    ya_ref[...] = ((_dot(diff, pw_ref[...]) + pb_ref[...]) * ps_ref[...]).astype(BF16)

    for r in range(1, 8):
        ph_ref[r - 1] = xg_ref[r:r + PH_ROWS, :]
    cw = cw_ref[...]
    base = HALO - (CONV_K - 1)

    def conv_rows(c0):
        acc = jnp.zeros((CONV_CHUNK, CONV_W), F32) + cb_ref[...]
        for j in range(CONV_K):
            off = base + j
            r = off % 8
            rows = pl.ds(c0 + (off - r), CONV_CHUNK)
            tap = xg_ref[rows, :] if r == 0 else ph_ref[r - 1, rows, :]
            acc = acc + tap * cw[j:j + 1, :]
        mu = jnp.mean(acc, axis=-1, keepdims=True)
        cen = acc - mu
        var = jnp.mean(cen * cen, axis=-1, keepdims=True)
        y = cen * lax.rsqrt(var + EPS) * lg_ref[...] + lb_ref[...]
        yb_ref[c0:c0 + CONV_CHUNK, :] = (y * jax.nn.sigmoid(y)).astype(BF16)

    fg = _dot(u, wf_ref[...]) + bf_ref[...]
    logf = jnp.minimum(fg, 0.0) - jnp.log1p(jnp.exp(-jnp.abs(fg)))
    l_hi, l_lo = _split_bf16(logf)
    cum2 = _dot(tri_ref[...], jnp.concatenate([l_hi, l_lo], axis=1))
    cum = cum2[:, :LANES] + cum2[:, LANES:] + carry_ref[...]
    carry_ref[...] = cum[TM - 1:TM, :]

    lane = lax.broadcasted_iota(jnp.int32, (TM, LANES), 1)
    fv = jnp.where(lane < HEADS, cum * LOG2E, 0.0)
    f_hi = fv.astype(BF16).astype(F32)
    r1 = fv - f_hi
    f_mid = r1.astype(BF16).astype(F32)
    f_lo = (r1 - f_mid).astype(BF16).astype(F32)
    fcat = f_hi + pltpu.roll(f_mid, HEADS, 1) + pltpu.roll(f_lo, 2 * HEADS, 1)
    fcat = jnp.where(lane == N_FPIECE * HEADS, 1.0, fcat).astype(BF16)

    for c0 in range(0, TM, CONV_CHUNK):
        conv_rows(c0)

    qkc = _dot(u, wqk_ref[...])
    bias_cols = _dot(fcat, sel_ref[...])
    from_proj = lane < HEAD_DIM
    for part, out_ref in enumerate((q_ref, k_ref)):
        for hd in range(HEADS):
            src = qkc[:, part * FOX_W + (hd // 2) * LANES:part * FOX_W + (hd // 2 + 1) * LANES]
            if hd % 2 == 1:
                src = pltpu.roll(src, HEAD_DIM, 1)
            tile = slice(hd * HEAD_PAD, (hd + 1) * HEAD_PAD)
            bias = bias_cols[:, part * ATT_W + hd * HEAD_PAD:part * ATT_W + (hd + 1) * HEAD_PAD]
            out_ref[:, tile] = jnp.where(from_proj, src, bias).astype(BF16)

    xa_ref[0:HALO, :] = xa_ref[TM:TM + HALO, :]
    xg_ref[0:HALO, :] = xg_ref[TM:TM + HALO, :]

    vt = _dot_nt(wvt_ref[...], u)
    row = lax.broadcasted_iota(jnp.int32, (ATT_W, TM), 0)
    vt_ref[0] = jnp.where(row % HEAD_PAD == HEAD_DIM, 1.0, vt).astype(BF16)


def _in_proj(src, g, wabc, wqk, wvt, wf, bf, tri, sel, pool_conv, batch):
    first_layer = isinstance(src, tuple)
    tp = batch * LP
    nblk = tp // TM
    row = lambda w: pl.BlockSpec((TM, w), lambda i: (i, 0))
    const = lambda x: pl.BlockSpec(x.shape, lambda i: (0, 0))
    if first_layer:
        x, meta = src
        seq_x = x.shape[0] // batch

        def x_rows(i):
            start = jnp.clip((i % BLOCKS_PER_SEQ) * TM - N_META, 0, seq_x - TM)
            return pl.multiple_of((i // BLOCKS_PER_SEQ) * seq_x + start, N_META), 0

        src_args = [x, meta]
        src_specs = [pl.BlockSpec((pl.Element(TM), pl.Element(D)), x_rows), const(meta)]
        h_spec, h_shape = [row(D)], [jax.ShapeDtypeStruct((tp, D), F32)]
    else:
        src_args, src_specs, h_spec, h_shape = [src], [row(D)], [], []
    return pl.pallas_call(
        functools.partial(_in_proj_kernel, first_layer),
        grid=(nblk,),
        in_specs=src_specs + [const(g), const(wabc), const(wqk), const(wvt), const(wf), const(bf),
                              const(tri), const(sel)] + [const(p) for p in pool_conv],
        out_specs=[row(POOL_W), row(CONV_W), row(ATT_W), row(ATT_W),
                   pl.BlockSpec((1, ATT_W, TM), lambda i: (i // BLOCKS_PER_SEQ, 0, i % BLOCKS_PER_SEQ))] + h_spec,
        out_shape=[jax.ShapeDtypeStruct((tp, POOL_W), BF16), jax.ShapeDtypeStruct((tp, CONV_W), BF16),
                   jax.ShapeDtypeStruct((tp, ATT_W), BF16), jax.ShapeDtypeStruct((tp, ATT_W), BF16),
                   jax.ShapeDtypeStruct((batch, ATT_W, LP), BF16)] + h_shape,
        scratch_shapes=[pltpu.VMEM((1, LANES), F32),
                        pltpu.VMEM((TM + HALO, POOL_W), F32), pltpu.VMEM((TM + HALO, CONV_W), F32),
                        pltpu.VMEM((7, PH_ROWS, CONV_W), F32)],
        compiler_params=pltpu.CompilerParams(dimension_semantics=("arbitrary",),
                                             vmem_limit_bytes=VMEM_LIMIT),
        name="in_proj",
    )(*src_args, g, wabc, wqk, wvt, wf, bf, tri, sel, *pool_conv)


HEADS_PER_STEP = 4


def _attn_kernel(q_ref, k_ref, vt_ref, o_ref, m_ref, acc_ref, s_ref, mt_ref):
    qi = pl.program_id(2)
    m_ref[...] = jnp.full_like(m_ref, -jnp.inf)
    acc_ref[...] = jnp.zeros_like(acc_ref)

    half = TQ // 2

    def causal(s, first_query):
        key = lax.broadcasted_iota(jnp.int32, s.shape, 0)
        qry = first_query + lax.broadcasted_iota(jnp.int32, s.shape, 1)
        return jnp.where(key <= qry, s, -jnp.inf)

    def scores(j, slot, masked=False):
        start = pl.multiple_of(j * TQ, TQ)
        for hd in range(HEADS_PER_STEP):
            cols = slice(hd * HEAD_PAD, (hd + 1) * HEAD_PAD)
            if masked:
                s_l = causal(_dot_nt(k_ref[pl.ds(start, half), cols], q_ref[0:half, cols]), 0)
                s_r = causal(_dot_nt(k_ref[pl.ds(start, TQ), cols], q_ref[half:, cols]), half)
                s_ref[slot, hd, 0:half, 0:half] = s_l
                s_ref[slot, hd, :, half:] = s_r
                mt_ref[slot, hd] = jnp.concatenate([jnp.max(s_l, axis=0, keepdims=True),
                                                    jnp.max(s_r, axis=0, keepdims=True)], axis=1)
            else:
                s = _dot_nt(k_ref[pl.ds(start, TQ), cols], q_ref[:, cols])
                s_ref[slot, hd] = s
                mt_ref[slot, hd] = jnp.max(s, axis=0, keepdims=True)

    def values(j, slot, masked=False):
        start = pl.multiple_of(j * TQ, TQ)
        for hd in range(HEADS_PER_STEP):
            cols = slice(hd * HEAD_PAD, (hd + 1) * HEAD_PAD)
            m_old = m_ref[hd]
            m_new = jnp.maximum(m_old, mt_ref[slot, hd])
            alpha = jnp.exp2(m_old - m_new)
            if masked:
                p_l = jnp.exp2(s_ref[slot, hd, 0:half, 0:half] - m_new[:, 0:half]).astype(BF16)
                p_r = jnp.exp2(s_ref[slot, hd, :, half:] - m_new[:, half:]).astype(BF16)
                pv = jnp.concatenate([_dot(vt_ref[0, cols, pl.ds(start, half)], p_l),
                                      _dot(vt_ref[0, cols, pl.ds(start, TQ)], p_r)], axis=1)
            else:
                p = jnp.exp2(s_ref[slot, hd] - m_new).astype(BF16)
                pv = _dot(vt_ref[0, cols, pl.ds(start, TQ)], p)
            acc_ref[hd] = alpha * acc_ref[hd] + pv
            m_ref[hd] = m_new

    @pl.when(qi == 0)
    def _():
        scores(0, 0, masked=True)
        values(0, 0, masked=True)

    @pl.when(qi > 0)
    def _():
        scores(0, 0)

        def two_tiles(j):
            scores(j + 1, 1)
            values(j, 0)
            scores(j + 2, 0)
            values(j + 1, 1)

        def body(t, c):
            two_tiles(4 * t)
            two_tiles(4 * t + 2)
            return c

        last = qi - 1
        n_pairs = last // 2
        lax.fori_loop(0, n_pairs // 2, body, 0)

        @pl.when(n_pairs % 2 == 1)
        def _():
            two_tiles(2 * (n_pairs - 1))

        @pl.when(last % 2 == 0)
        def _():
            scores(qi, 1, masked=True)
            values(last, 0)
            values(qi, 1, masked=True)

        @pl.when(last % 2 == 1)
        def _():
            scores(last, 1)
            values(last - 1, 0)
            scores(qi, 0, masked=True)
            values(last, 1)
            values(qi, 0, masked=True)

    outs = []
    for hd in range(HEADS_PER_STEP):
        acc = acc_ref[hd]
        outs.append(acc[:HEAD_DIM, :] / acc[HEAD_DIM:HEAD_DIM + 1, :])
    o_ref[...] = jnp.concatenate(outs, axis=0).T.astype(BF16)


def _attention(q, k, vt, batch):
    nq = LP // TQ
    w = HEADS_PER_STEP * HEAD_PAD
    return pl.pallas_call(
        _attn_kernel,
        grid=(batch, HEADS // HEADS_PER_STEP, nq),
        in_specs=[pl.BlockSpec((TQ, w), lambda b, h, i: (b * nq + i, h)),
                  pl.BlockSpec((LP, w), lambda b, h, i: (b, h)),
                  pl.BlockSpec((1, w, LP), lambda b, h, i: (b, h, 0))],
        out_specs=pl.BlockSpec((TQ, HEADS_PER_STEP * HEAD_DIM), lambda b, h, i: (b * nq + i, h)),
        out_shape=jax.ShapeDtypeStruct((batch * LP, FOX_W), BF16),
        scratch_shapes=[pltpu.VMEM((HEADS_PER_STEP, 1, TQ), F32),
                        pltpu.VMEM((HEADS_PER_STEP, HEAD_PAD, TQ), F32),
                        pltpu.VMEM((2, HEADS_PER_STEP, TQ, TQ), F32),
                        pltpu.VMEM((2, HEADS_PER_STEP, 1, TQ), F32)],
        compiler_params=pltpu.CompilerParams(dimension_semantics=("arbitrary", "arbitrary", "arbitrary"),
                                             vmem_limit_bytes=VMEM_LIMIT),
        name="fox_attention",
    )(q, k, vt)


TM_OUT = 1024


def _mixer_out_kernel(h_ref, g_ref, ya_ref, yb_ref, o_ref, wg_ref,
                      woa_ref, wob_ref, woc_ref, wo_ref, out_ref):
    h = h_ref[...]
    u = _rms_norm(h, g_ref[...]).astype(BF16)
    branches = (_dot(ya_ref[...], woa_ref[...]), _dot(yb_ref[...], wob_ref[...]),
                _dot(o_ref[...], woc_ref[...]))
    merged = None
    for n, y in enumerate(branches):
        gated = jax.nn.sigmoid(_dot(u, wg_ref[:, n * D:(n + 1) * D])) * y
        merged = gated if merged is None else merged + gated
    out_ref[...] = h + _dot(merged.astype(BF16), wo_ref[...])


def _mixer_out(h, g, ya, yb, o, wg, woa, wob, woc, wo):
    tp = h.shape[0]
    nblk = tp // TM_OUT
    row = lambda w: pl.BlockSpec((TM_OUT, w), lambda i: (i, 0))
    const = lambda x: pl.BlockSpec(x.shape, lambda i: (0, 0), pipeline_mode=pl.Buffered(1))
    return pl.pallas_call(
        _mixer_out_kernel,
        grid=(nblk,),
        in_specs=[row(D), const(g), row(POOL_W), row(CONV_W), row(FOX_W),
                  const(wg), const(woa), const(wob), const(woc), const(wo)],
        out_specs=row(D),
        out_shape=jax.ShapeDtypeStruct((tp, D), F32),
        compiler_params=pltpu.CompilerParams(dimension_semantics=("arbitrary",),
                                             vmem_limit_bytes=VMEM_LIMIT),
        name="mixer_out",
    )(h, g, ya, yb, o, wg, woa, wob, woc, wo)


TS = 1024
SEG_SIZES = (256, 320, 384, 448, 512)
SEG_ALIGN = 64
SLOT_LANE = EPG


def _route(lt):
    gl = [lt[g:g + 1, :] for g in range(N_GROUPS)]
    gmax = functools.reduce(jnp.maximum, gl)
    gsum = functools.reduce(lambda x, y: x + y, [jnp.exp(x - gmax) for x in gl])
    gw = 1.0 / gsum
    sel, taken = [], jnp.zeros_like(gmax, dtype=jnp.bool_)
    for g in range(N_GROUPS):
        s = jnp.logical_and(gl[g] == gmax, jnp.logical_not(taken))
        sel.append(s)
        taken = jnp.logical_or(taken, s)
    el = []
    for j in range(EPG):
        e = jnp.zeros_like(gmax)
        for g in range(N_GROUPS):
            e = jnp.where(sel[g], lt[N_GROUPS + g * EPG + j:N_GROUPS + g * EPG + j + 1, :], e)
        el.append(e)
    neg = jnp.full_like(gmax, -jnp.inf)
    top1 = functools.reduce(jnp.maximum, el)
    is1, taken = [], jnp.zeros_like(gmax, dtype=jnp.bool_)
    for j in range(EPG):
        s = jnp.logical_and(el[j] == top1, jnp.logical_not(taken))
        is1.append(s)
        taken = jnp.logical_or(taken, s)
    rest = [jnp.where(is1[j], neg, el[j]) for j in range(EPG)]
    top2 = functools.reduce(jnp.maximum, rest)
    is2, taken = [], jnp.zeros_like(gmax, dtype=jnp.bool_)
    for j in range(EPG):
        s = jnp.logical_and(jnp.logical_and(rest[j] == top2, jnp.logical_not(is1[j])),
                            jnp.logical_not(taken))
        is2.append(s)
        taken = jnp.logical_or(taken, s)
    w1 = 1.0 / (1.0 + jnp.exp(top2 - top1))
    w2 = 1.0 - w1
    comb = [jnp.where(is1[j], w1, jnp.where(is2[j], w2, 0.0)) * gw for j in range(EPG)]
    return sel, comb


def _moe_sort_kernel(h_ref, g_ref, wr2_ref, rb_ref, tri_ref, xs_ref, cs_ref, info_ref, seg_ref):
    vn = _rms_norm(h_ref[...], g_ref[...])
    v_hi, v_lo = _split_bf16(vn)
    wr2 = wr2_ref[...]
    r2 = _dot(v_hi, wr2)
    logits = r2[:, :LANES] + r2[:, LANES:] + _dot(v_lo, wr2[:, :LANES]) + rb_ref[...]
    sel, comb = _route(logits.T)

    onehot = jnp.concatenate([s.astype(F32) for s in sel] + [jnp.zeros((8 - N_GROUPS, TS), F32)], axis=0)
    cnt = _dot(onehot.astype(BF16), tri_ref[...])
    totals = [jnp.max(cnt[g:g + 1, :], axis=1, keepdims=True) for g in range(N_GROUPS)]
    starts = [jnp.zeros((1, 1), F32)]
    for g in range(N_GROUPS - 1):
        starts.append(starts[g] + totals[g])
    slot = jnp.zeros((1, TS), F32)
    for g in range(N_GROUPS):
        slot = jnp.where(sel[g], starts[g] + cnt[g:g + 1, :] - 1.0, slot)

    info = jnp.concatenate(comb + [slot, jnp.zeros((LANES - EPG - 1, TS), F32)], axis=0).T
    info_ref[...] = info

    srow = lax.broadcasted_iota(jnp.int32, (TS, TS), 0)
    perm = jnp.where(srow == slot.astype(jnp.int32), 1.0, 0.0).astype(BF16)
    lane = lax.broadcasted_iota(jnp.int32, (TS, LANES), 1)
    c_hi = info.astype(BF16).astype(F32)
    c_lo = (info - c_hi).astype(BF16).astype(F32)
    pack = jnp.where(lane < EPG, c_hi, pltpu.roll(c_lo, EPG, 1)).astype(BF16)
    xs_ext = _dot(perm, jnp.concatenate([v_hi, pack], axis=1))
    xs_ref[...] = xs_ext[:, :D].astype(BF16)
    cs = xs_ext[:, D:]
    cs_ref[...] = cs + pltpu.roll(cs, LANES - EPG, 1)

    lane1 = lax.broadcasted_iota(jnp.int32, (1, LANES), 1)
    seg = jnp.zeros((1, LANES), F32)
    for g in range(N_GROUPS):
        seg = jnp.where(lane1 == g, starts[g], seg)
        seg = jnp.where(lane1 == N_GROUPS + g, starts[g] + totals[g], seg)
    seg_ref[0] = seg.astype(jnp.int32)


def _moe_sort(h, g, wr2, rb, tri):
    tp = h.shape[0]
    nblk = tp // TS
    const = lambda x: pl.BlockSpec(x.shape, lambda i: (0, 0))
    row = lambda w: pl.BlockSpec((TS, w), lambda i: (i, 0))
    return pl.pallas_call(
        _moe_sort_kernel,
        grid=(nblk,),
        in_specs=[row(D), const(g), const(wr2), const(rb), const(tri)],
        out_specs=[row(D), row(LANES), row(LANES), pl.BlockSpec((1, 1, LANES), lambda i: (i, 0, 0))],
        out_shape=[jax.ShapeDtypeStruct((tp, D), BF16), jax.ShapeDtypeStruct((tp, LANES), F32),
                   jax.ShapeDtypeStruct((tp, LANES), F32), jax.ShapeDtypeStruct((nblk, 1, LANES), jnp.int32)],
        compiler_params=pltpu.CompilerParams(dimension_semantics=("arbitrary",),
                                             vmem_limit_bytes=VMEM_LIMIT),
        name="moe_sort",
    )(h, g, wr2, rb, tri)


def _moe_ffn_kernel(seg_ref, xs_ref, cs_ref, info_ref, h_ref, w1_ref, w3_ref, w2_ref, out_ref, ys_ref):
    i = pl.program_id(0)
    ys_ref[...] = jnp.zeros_like(ys_ref)

    def window(g, start, end, lo, size):
        ws = pl.multiple_of(jnp.minimum(lo, TS - size), SEG_ALIGN)
        rows = pl.ds(ws, size)
        x = xs_ref[rows, :]
        h1 = _dot(x, w1_ref[g])
        h3 = _dot(x, w3_ref[g])
        he = h1 * jax.nn.sigmoid(h1) * h3
        r = ws + lax.broadcasted_iota(jnp.int32, (size, 1), 0)
        inseg = jnp.logical_and(r >= jnp.maximum(start, lo), r < jnp.minimum(end, lo + size))
        cs = cs_ref[rows, :]
        parts = []
        for j in range(EPG):
            wcol = jnp.where(inseg, cs[:, j:j + 1], 0.0)
            parts.append((he[:, j * EXPERT_HIDDEN:(j + 1) * EXPERT_HIDDEN] * wcol).astype(BF16))
        ys_ref[rows, :] += _dot(jnp.concatenate(parts, axis=1), w2_ref[g])

    def group(g, carry):
        start = seg_ref[i * LANES + g]
        end = seg_ref[i * LANES + N_GROUPS + g]
        first = (start // SEG_ALIGN) * SEG_ALIGN
        need = jnp.where(end > start, end - first, 0)
        n_full = jnp.maximum(need - 1, 0) // SEG_SIZES[-1]

        def full(w, c2):
            window(g, start, end, first + w * SEG_SIZES[-1], SEG_SIZES[-1])
            return c2

        lax.fori_loop(0, n_full, full, 0)
        rest = need - n_full * SEG_SIZES[-1]
        for n, size in enumerate(SEG_SIZES):
            fits = jnp.logical_and(rest > (SEG_SIZES[n - 1] if n else 0), rest <= size)

            @pl.when(fits)
            def _():
                window(g, start, end, first + n_full * SEG_SIZES[-1], size)

        return carry

    lax.fori_loop(0, N_GROUPS, group, 0)

    slot = info_ref[:, SLOT_LANE:SLOT_LANE + 1].astype(jnp.int32)
    scol = lax.broadcasted_iota(jnp.int32, (TS, TS), 1)
    unperm = jnp.where(scol == slot, 1.0, 0.0).astype(BF16)
    out_ref[...] = h_ref[...] + _dot(unperm, ys_ref[...].astype(BF16))


def _moe_ffn(seg, xs, cs, info, h, w1g, w3g, w2g):
    tp = h.shape[0]
    nblk = tp // TS
    row = lambda w: pl.BlockSpec((TS, w), lambda i, s: (i, 0))
    resident = lambda x: pl.BlockSpec(x.shape, lambda i, s: (0, 0, 0), pipeline_mode=pl.Buffered(1))
    return pl.pallas_call(
        _moe_ffn_kernel,
        grid_spec=pltpu.PrefetchScalarGridSpec(
            num_scalar_prefetch=1,
            grid=(nblk,),
            in_specs=[row(D), row(LANES), row(LANES), row(D), resident(w1g), resident(w3g), resident(w2g)],
            out_specs=row(D),
            scratch_shapes=[pltpu.VMEM((TS, D), F32)]),
        out_shape=jax.ShapeDtypeStruct((tp, D), F32),
        compiler_params=pltpu.CompilerParams(dimension_semantics=("arbitrary",),
                                             vmem_limit_bytes=MOE_VMEM_LIMIT),
        name="moe_ffn",
    )(seg, xs, cs, info, h, w1g, w3g, w2g)


def _final_norm_kernel(h_ref, g_ref, out_ref):
    out_ref[...] = _rms_norm(h_ref[...], g_ref[...])


def _final_norm(h, g, batch, seq_out):
    nblk = seq_out // TM
    return pl.pallas_call(
        _final_norm_kernel,
        grid=(batch, nblk),
        in_specs=[pl.BlockSpec((pl.Element(TM), pl.Element(D)),
                               lambda b, r: (pl.multiple_of(b * LP + N_META + r * TM, N_META), 0)),
                  pl.BlockSpec((1, D), lambda b, r: (0, 0))],
        out_specs=pl.BlockSpec((TM, D), lambda b, r: (b * nblk + r, 0)),
        out_shape=jax.ShapeDtypeStruct((batch * seq_out, D), F32),
        compiler_params=pltpu.CompilerParams(dimension_semantics=("arbitrary", "arbitrary")),
        name="final_norm",
    )(h, g)


def _pad_lanes(x, n=LANES):
    return jnp.pad(x, ((0, 0), (0, n - x.shape[1])))


def _pad_rows(w, n):
    return jnp.pad(w, ((0, n - w.shape[0]), (0, 0)))


def _hi_lo_weight(w):
    hi = w.astype(BF16)
    lo = (w - hi.astype(F32)).astype(BF16)
    return jnp.concatenate([_pad_lanes(hi), _pad_lanes(lo)], axis=1)


def _per_head_cols(w):
    k = w.shape[0]
    w = w.reshape(k, HEADS, HEAD_DIM)
    return jnp.pad(w, ((0, 0), (0, 0), (0, HEAD_PAD - HEAD_DIM))).reshape(k, ATT_W)


def _bias_select():
    sel = jnp.zeros((LANES, 2, HEADS, HEAD_PAD), F32)
    one_lane = N_FPIECE * HEADS
    for p in range(N_FPIECE):
        for hd in range(HEADS):
            sel = sel.at[p * HEADS + hd, 0, hd, HEAD_DIM + p].set(1.0)
            sel = sel.at[p * HEADS + hd, 1, hd, HEAD_DIM + N_FPIECE + p].set(-1.0)
    sel = sel.at[one_lane, 0, :, HEAD_DIM + N_FPIECE:HEAD_DIM + 2 * N_FPIECE].set(1.0)
    sel = sel.at[one_lane, 1, :, HEAD_DIM:HEAD_DIM + N_FPIECE].set(1.0)
    return sel.reshape(LANES, 2 * ATT_W).astype(BF16)


def kernel(x, meta, norm1_g, w_in, b_forget, pool_w, pool_b, pool_scale, conv_w, conv_b, conv_ln_g,
           conv_ln_b, w_out_a, w_out_b, w_out_c, w_o, norm2_g, router_g, router_g_b, router_e,
           router_e_b, exp_w1, exp_w3, exp_w2, final_g):
    B, L, _ = x.shape
    depth = w_in.shape[0]
    assert LP == L + TM and LP % TQ == 0 and meta.shape[0] == N_META
    assert (B * LP) % TS == 0 and (B * LP) % TM_OUT == 0
    h = (x.reshape(B * L, D), meta.astype(x.dtype))
    tri = jnp.tril(jnp.ones((TM, TM), BF16))
    tri_u = jnp.triu(jnp.ones((TS, TS), BF16))
    sel = _bias_select()

    c_abc = POOL_W + 2 * CONV_W
    c_q = c_abc + FOX_W
    c_k = c_q + FOX_W
    c_v = c_k + FOX_W
    c_f = c_v + HEADS
    for l in range(depth):
        wl = w_in[l]
        wabc = wl[:, :c_abc].astype(BF16)
        wqk = jnp.concatenate([wl[:, c_abc:c_q] * (HEAD_DIM ** -0.5 * LOG2E), wl[:, c_q:c_k]],
                              axis=1).astype(BF16)
        wvt = _per_head_cols(wl[:, c_k:c_v]).T.astype(BF16)
        wf = _pad_lanes(wl[:, c_v:c_f]).astype(BF16)
        wg = wl[:, c_f:].astype(BF16)
        bf = _pad_lanes(b_forget[l][None, :])
        pw = jax.scipy.linalg.block_diag(*[pool_w[l, g] for g in range(pool_w.shape[1])]).astype(BF16)
        pool_conv = (pw, pool_b[l].reshape(1, POOL_W), pool_scale[l][None, :], _pad_rows(conv_w[l], 32),
                     conv_b[l][None, :], conv_ln_g[l][None, :], conv_ln_b[l][None, :])
        ya, yb, q, k, vt, *h0 = _in_proj(h, norm1_g[l][None, :], wabc, wqk, wvt, wf, bf, tri, sel, pool_conv, B)
        if h0:
            h = h0[0]
        o = _attention(q, k, vt, B)
        h = _mixer_out(h, norm1_g[l][None, :], ya, yb, o, wg, w_out_a[l].astype(BF16),
                       w_out_b[l].astype(BF16), w_out_c[l].astype(BF16), w_o[l].astype(BF16))

        wr2 = _hi_lo_weight(jnp.concatenate([router_g[l], router_e[l]], axis=1))
        rb = _pad_lanes(jnp.concatenate([router_g_b[l], router_e_b[l]])[None, :])
        grouped = lambda w: (w.reshape(N_GROUPS, EPG, D, EXPERT_HIDDEN).transpose(0, 2, 1, 3)
                             .reshape(N_GROUPS, D, GROUP_HIDDEN).astype(BF16))
        xs, cs, info, seg = _moe_sort(h, norm2_g[l][None, :], wr2, rb, tri_u)
        h = _moe_ffn(seg.reshape(-1), xs, cs, info, h, grouped(exp_w1[l]), grouped(exp_w3[l]),
                     exp_w2[l].reshape(N_GROUPS, GROUP_HIDDEN, D).astype(BF16))

    assert L % TM == 0
    return _final_norm(h, final_g[None, :], B, L).reshape(B, L, D)
```

```python
import functools
import math

import jax
import jax.numpy as jnp
from jax import lax
from jax.experimental import pallas as pl
from jax.experimental.pallas import tpu as pltpu

D = 1024
N_META = 16
EPS = 1e-6
POOL_W = 256
POOL_WINDOWS = (2, 4, 8, 16)
CONV_W = 256
CONV_K = 31
HEADS = 8
HEAD_DIM = 64
FOX_W = HEADS * HEAD_DIM
N_GROUPS = 4
EPG = 4
EXPERT_HIDDEN = 256
GROUP_HIDDEN = EPG * EXPERT_HIDDEN

LANES = 128
HEAD_PAD = LANES
ATT_W = HEADS * HEAD_PAD
LP = 8704
TM = 512
BLOCKS_PER_SEQ = LP // TM
HALO = 32
TQ = 512
N_FPIECE = 3
VMEM_LIMIT = 56 * 1024 * 1024
MOE_VMEM_LIMIT = 60 * 1024 * 1024
LOG2E = math.log2(math.e)

F32 = jnp.float32
BF16 = jnp.bfloat16


def _dot(a, b):
    return jnp.dot(a, b, preferred_element_type=F32)


def _dot_nt(a, b):
    return lax.dot_general(a, b, (((1,), (1,)), ((), ())), preferred_element_type=F32)


def _rms_norm(x, g):
    ms = jnp.mean(x * x, axis=-1, keepdims=True)
    return x * lax.rsqrt(ms + EPS) * g


def _split_bf16(x):
    hi = x.astype(BF16)
    lo = (x - hi.astype(F32)).astype(BF16)
    return hi, lo


CONV_CHUNK = 64
PH_ROWS = TM + HALO - 8


def _in_proj_kernel(first_layer, *refs):
    if first_layer:
        x_ref, meta_ref, *refs = refs
    else:
        h_ref, *refs = refs
    (g_ref, wabc_ref, wqk_ref, wvt_ref, wf_ref, bf_ref, tri_ref, sel_ref,
     pw_ref, pb_ref, ps_ref, cw_ref, cb_ref, lg_ref, lb_ref,
     ya_ref, yb_ref, q_ref, k_ref, vt_ref, *refs) = refs
    if first_layer:
        h_ref, *refs = refs
    carry_ref, xa_ref, xg_ref, ph_ref = refs
    i = pl.program_id(0)
    blk = i % BLOCKS_PER_SEQ

    if first_layer:
        @pl.when(blk == 0)
        def _():
            h_ref[0:N_META, :] = meta_ref[...]
            h_ref[N_META:, :] = x_ref[0:TM - N_META, :]

        @pl.when(blk == BLOCKS_PER_SEQ - 1)
        def _():
            h_ref[0:N_META, :] = x_ref[TM - N_META:, :]
            h_ref[N_META:, :] = jnp.zeros((TM - N_META, D), F32)

        @pl.when(jnp.logical_and(blk > 0, blk < BLOCKS_PER_SEQ - 1))
        def _():
            h_ref[...] = x_ref[...]

    @pl.when(blk == 0)
    def _():
        carry_ref[...] = jnp.zeros_like(carry_ref)
        xa_ref[0:HALO, :] = jnp.zeros((HALO, POOL_W), F32)
        xg_ref[0:HALO, :] = jnp.zeros((HALO, CONV_W), F32)

    u = _rms_norm(h_ref[...], g_ref[...]).astype(BF16)

    abc = _dot(u, wabc_ref[...])
    xa_ref[HALO:, :] = abc[:, :POOL_W]
    xg_ref[HALO:, :] = abc[:, POOL_W:POOL_W + CONV_W] * jax.nn.sigmoid(abc[:, POOL_W + CONV_W:])

    pos = blk * TM + lax.broadcasted_iota(jnp.int32, (TM, LANES), 0) + 1
    lane = lax.broadcasted_iota(jnp.int32, (TM, LANES), 1)
    low = lane < (LANES // 2)
    means = []
    for t in range(2):
        x = xa_ref[:, t * LANES:(t + 1) * LANES]
        s2 = x + pltpu.roll(x, 1, 0)
        s4 = s2 + pltpu.roll(s2, 2, 0)
        if t == 0:
            small, big = s2, s4
        else:
            s8 = s4 + pltpu.roll(s4, 4, 0)
            small, big = s8, s8 + pltpu.roll(s8, 8, 0)
        w_small, w_big = POOL_WINDOWS[2 * t], POOL_WINDOWS[2 * t + 1]
        cnt = jnp.minimum(pos, jnp.where(low, w_small, w_big)).astype(F32)
        means.append(jnp.where(low, small[HALO:], big[HALO:]) / cnt - x[HALO:])
    diff = jnp.concatenate(means, axis=1).astype(BF16)
    ya_ref[...] = ((_dot(diff, pw_ref[...]) + pb_ref[...]) * ps_ref[...]).astype(BF16)

    for r in range(1, 8):
        ph_ref[r - 1] = xg_ref[r:r + PH_ROWS, :]
    cw = cw_ref[...]
    base = HALO - (CONV_K - 1)

    def conv_rows(c0):
        acc = jnp.zeros((CONV_CHUNK, CONV_W), F32) + cb_ref[...]
        for j in range(CONV_K):
            off = base + j
            r = off % 8
            rows = pl.ds(c0 + (off - r), CONV_CHUNK)
            tap = xg_ref[rows, :] if r == 0 else ph_ref[r - 1, rows, :]
            acc = acc + tap * cw[j:j + 1, :]
        mu = jnp.mean(acc, axis=-1, keepdims=True)
        cen = acc - mu
        var = jnp.mean(cen * cen, axis=-1, keepdims=True)
        y = cen * lax.rsqrt(var + EPS) * lg_ref[...] + lb_ref[...]
        yb_ref[c0:c0 + CONV_CHUNK, :] = (y * jax.nn.sigmoid(y)).astype(BF16)

    fg = _dot(u, wf_ref[...]) + bf_ref[...]
    logf = jnp.minimum(fg, 0.0) - jnp.log1p(jnp.exp(-jnp.abs(fg)))
    l_hi, l_lo = _split_bf16(logf)
    cum2 = _dot(tri_ref[...], jnp.concatenate([l_hi, l_lo], axis=1))
    cum = cum2[:, :LANES] + cum2[:, LANES:] + carry_ref[...]
    carry_ref[...] = cum[TM - 1:TM, :]

    lane = lax.broadcasted_iota(jnp.int32, (TM, LANES), 1)
    fv = jnp.where(lane < HEADS, cum * LOG2E, 0.0)
    f_hi = fv.astype(BF16).astype(F32)
    r1 = fv - f_hi
    f_mid = r1.astype(BF16).astype(F32)
    f_lo = (r1 - f_mid).astype(BF16).astype(F32)
    fcat = f_hi + pltpu.roll(f_mid, HEADS, 1) + pltpu.roll(f_lo, 2 * HEADS, 1)
    fcat = jnp.where(lane == N_FPIECE * HEADS, 1.0, fcat).astype(BF16)

    for c0 in range(0, TM, CONV_CHUNK):
        conv_rows(c0)

    qkc = _dot(u, wqk_ref[...])
    bias_cols = _dot(fcat, sel_ref[...])
    from_proj = lane < HEAD_DIM
    for part, out_ref in enumerate((q_ref, k_ref)):
        for hd in range(HEADS):
            src = qkc[:, part * FOX_W + (hd // 2) * LANES:part * FOX_W + (hd // 2 + 1) * LANES]
            if hd % 2 == 1:
                src = pltpu.roll(src, HEAD_DIM, 1)
            tile = slice(hd * HEAD_PAD, (hd + 1) * HEAD_PAD)
            bias = bias_cols[:, part * ATT_W + hd * HEAD_PAD:part * ATT_W + (hd + 1) * HEAD_PAD]
            out_ref[:, tile] = jnp.where(from_proj, src, bias).astype(BF16)

    xa_ref[0:HALO, :] = xa_ref[TM:TM + HALO, :]
    xg_ref[0:HALO, :] = xg_ref[TM:TM + HALO, :]

    vt = _dot_nt(wvt_ref[...], u)
    row = lax.broadcasted_iota(jnp.int32, (ATT_W, TM), 0)
    vt_ref[0] = jnp.where(row % HEAD_PAD == HEAD_DIM, 1.0, vt).astype(BF16)


def _in_proj(src, g, wabc, wqk, wvt, wf, bf, tri, sel, pool_conv, batch):
    first_layer = isinstance(src, tuple)
    tp = batch * LP
    nblk = tp // TM
    row = lambda w: pl.BlockSpec((TM, w), lambda i: (i, 0))
    const = lambda x: pl.BlockSpec(x.shape, lambda i: (0, 0))
    if first_layer:
        x, meta = src
        seq_x = x.shape[0] // batch

        def x_rows(i):
            start = jnp.clip((i % BLOCKS_PER_SEQ) * TM - N_META, 0, seq_x - TM)
            return pl.multiple_of((i // BLOCKS_PER_SEQ) * seq_x + start, N_META), 0

        src_args = [x, meta]
        src_specs = [pl.BlockSpec((pl.Element(TM), pl.Element(D)), x_rows), const(meta)]
        h_spec, h_shape = [row(D)], [jax.ShapeDtypeStruct((tp, D), F32)]
    else:
        src_args, src_specs, h_spec, h_shape = [src], [row(D)], [], []
    return pl.pallas_call(
        functools.partial(_in_proj_kernel, first_layer),
        grid=(nblk,),
        in_specs=src_specs + [const(g), const(wabc), const(wqk), const(wvt), const(wf), const(bf),
                              const(tri), const(sel)] + [const(p) for p in pool_conv],
        out_specs=[row(POOL_W), row(CONV_W), row(ATT_W), row(ATT_W),
                   pl.BlockSpec((1, ATT_W, TM), lambda i: (i // BLOCKS_PER_SEQ, 0, i % BLOCKS_PER_SEQ))] + h_spec,
        out_shape=[jax.ShapeDtypeStruct((tp, POOL_W), BF16), jax.ShapeDtypeStruct((tp, CONV_W), BF16),
                   jax.ShapeDtypeStruct((tp, ATT_W), BF16), jax.ShapeDtypeStruct((tp, ATT_W), BF16),
                   jax.ShapeDtypeStruct((batch, ATT_W, LP), BF16)] + h_shape,
        scratch_shapes=[pltpu.VMEM((1, LANES), F32),
                        pltpu.VMEM((TM + HALO, POOL_W), F32), pltpu.VMEM((TM + HALO, CONV_W), F32),
                        pltpu.VMEM((7, PH_ROWS, CONV_W), F32)],
        compiler_params=pltpu.CompilerParams(dimension_semantics=("arbitrary",),
                                             vmem_limit_bytes=VMEM_LIMIT),
        name="in_proj",
    )(*src_args, g, wabc, wqk, wvt, wf, bf, tri, sel, *pool_conv)


HEADS_PER_STEP = 4


def _attn_kernel(q_ref, k_ref, vt_ref, o_ref, m_ref, acc_ref, s_ref, mt_ref):
    qi = pl.program_id(2)
    m_ref[...] = jnp.full_like(m_ref, -jnp.inf)
    acc_ref[...] = jnp.zeros_like(acc_ref)

    half = TQ // 2

    def causal(s, first_query):
        key = lax.broadcasted_iota(jnp.int32, s.shape, 0)
        qry = first_query + lax.broadcasted_iota(jnp.int32, s.shape, 1)
        return jnp.where(key <= qry, s, -jnp.inf)

    def scores(j, slot, masked=False):
        start = pl.multiple_of(j * TQ, TQ)
        for hd in range(HEADS_PER_STEP):
            cols = slice(hd * HEAD_PAD, (hd + 1) * HEAD_PAD)
            if masked:
                s_l = causal(_dot_nt(k_ref[pl.ds(start, half), cols], q_ref[0:half, cols]), 0)
                s_r = causal(_dot_nt(k_ref[pl.ds(start, TQ), cols], q_ref[half:, cols]), half)
                s_ref[slot, hd, 0:half, 0:half] = s_l
                s_ref[slot, hd, :, half:] = s_r
                mt_ref[slot, hd] = jnp.concatenate([jnp.max(s_l, axis=0, keepdims=True),
                                                    jnp.max(s_r, axis=0, keepdims=True)], axis=1)
            else:
                s = _dot_nt(k_ref[pl.ds(start, TQ), cols], q_ref[:, cols])
                s_ref[slot, hd] = s
                mt_ref[slot, hd] = jnp.max(s, axis=0, keepdims=True)

    def values(j, slot, masked=False):
        start = pl.multiple_of(j * TQ, TQ)
        for hd in range(HEADS_PER_STEP):
            cols = slice(hd * HEAD_PAD, (hd + 1) * HEAD_PAD)
            m_old = m_ref[hd]
            m_new = jnp.maximum(m_old, mt_ref[slot, hd])
            alpha = jnp.exp2(m_old - m_new)
            if masked:
                p_l = jnp.exp2(s_ref[slot, hd, 0:half, 0:half] - m_new[:, 0:half]).astype(BF16)
                p_r = jnp.exp2(s_ref[slot, hd, :, half:] - m_new[:, half:]).astype(BF16)
                pv = jnp.concatenate([_dot(vt_ref[0, cols, pl.ds(start, half)], p_l),
                                      _dot(vt_ref[0, cols, pl.ds(start, TQ)], p_r)], axis=1)
            else:
                p = jnp.exp2(s_ref[slot, hd] - m_new).astype(BF16)
                pv = _dot(vt_ref[0, cols, pl.ds(start, TQ)], p)
            acc_ref[hd] = alpha * acc_ref[hd] + pv
            m_ref[hd] = m_new

    @pl.when(qi == 0)
    def _():
        scores(0, 0, masked=True)
        values(0, 0, masked=True)

    @pl.when(qi > 0)
    def _():
        scores(0, 0)

        def two_tiles(j):
            scores(j + 1, 1)
            values(j, 0)
            scores(j + 2, 0)
            values(j + 1, 1)

        def body(t, c):
            two_tiles(4 * t)
            two_tiles(4 * t + 2)
            return c

        last = qi - 1
        n_pairs = last // 2
        lax.fori_loop(0, n_pairs // 2, body, 0)

        @pl.when(n_pairs % 2 == 1)
        def _():
            two_tiles(2 * (n_pairs - 1))

        @pl.when(last % 2 == 0)
        def _():
            scores(qi, 1, masked=True)
            values(last, 0)
            values(qi, 1, masked=True)

        @pl.when(last % 2 == 1)
        def _():
            scores(last, 1)
            values(last - 1, 0)
            scores(qi, 0, masked=True)
            values(last, 1)
            values(qi, 0, masked=True)

    outs = []
    for hd in range(HEADS_PER_STEP):
        acc = acc_ref[hd]
        outs.append(acc[:HEAD_DIM, :] / acc[HEAD_DIM:HEAD_DIM + 1, :])
    o_ref[...] = jnp.concatenate(outs, axis=0).T.astype(BF16)


def _attention(q, k, vt, batch):
    nq = LP // TQ
    w = HEADS_PER_STEP * HEAD_PAD
    return pl.pallas_call(
        _attn_kernel,
        grid=(batch, HEADS // HEADS_PER_STEP, nq),
        in_specs=[pl.BlockSpec((TQ, w), lambda b, h, i: (b * nq + i, h)),
                  pl.BlockSpec((LP, w), lambda b, h, i: (b, h)),
                  pl.BlockSpec((1, w, LP), lambda b, h, i: (b, h, 0))],
        out_specs=pl.BlockSpec((TQ, HEADS_PER_STEP * HEAD_DIM), lambda b, h, i: (b * nq + i, h)),
        out_shape=jax.ShapeDtypeStruct((batch * LP, FOX_W), BF16),
        scratch_shapes=[pltpu.VMEM((HEADS_PER_STEP, 1, TQ), F32),
                        pltpu.VMEM((HEADS_PER_STEP, HEAD_PAD, TQ), F32),
                        pltpu.VMEM((2, HEADS_PER_STEP, TQ, TQ), F32),
                        pltpu.VMEM((2, HEADS_PER_STEP, 1, TQ), F32)],
        compiler_params=pltpu.CompilerParams(dimension_semantics=("arbitrary", "arbitrary", "arbitrary"),
                                             vmem_limit_bytes=VMEM_LIMIT),
        name="fox_attention",
    )(q, k, vt)


TM_OUT = 1024


def _mixer_out_kernel(h_ref, g_ref, ya_ref, yb_ref, o_ref, wg_ref,
                      woa_ref, wob_ref, woc_ref, wo_ref, out_ref):
    h = h_ref[...]
    u = _rms_norm(h, g_ref[...]).astype(BF16)
    branches = (_dot(ya_ref[...], woa_ref[...]), _dot(yb_ref[...], wob_ref[...]),
                _dot(o_ref[...], woc_ref[...]))
    merged = None
    for n, y in enumerate(branches):
        gated = jax.nn.sigmoid(_dot(u, wg_ref[:, n * D:(n + 1) * D])) * y
        merged = gated if merged is None else merged + gated
    out_ref[...] = h + _dot(merged.astype(BF16), wo_ref[...])


def _mixer_out(h, g, ya, yb, o, wg, woa, wob, woc, wo):
    tp = h.shape[0]
    nblk = tp // TM_OUT
    row = lambda w: pl.BlockSpec((TM_OUT, w), lambda i: (i, 0))
    const = lambda x: pl.BlockSpec(x.shape, lambda i: (0, 0), pipeline_mode=pl.Buffered(1))
    return pl.pallas_call(
        _mixer_out_kernel,
        grid=(nblk,),
        in_specs=[row(D), const(g), row(POOL_W), row(CONV_W), row(FOX_W),
                  const(wg), const(woa), const(wob), const(woc), const(wo)],
        out_specs=row(D),
        out_shape=jax.ShapeDtypeStruct((tp, D), F32),
        compiler_params=pltpu.CompilerParams(dimension_semantics=("arbitrary",),
                                             vmem_limit_bytes=VMEM_LIMIT),
        name="mixer_out",
    )(h, g, ya, yb, o, wg, woa, wob, woc, wo)


TS = 1024
SEG_SIZES = (256, 320, 384, 448, 512)
SEG_ALIGN = 64
SLOT_LANE = EPG


def _route(lt):
    gl = [lt[g:g + 1, :] for g in range(N_GROUPS)]
    gmax = functools.reduce(jnp.maximum, gl)
    gsum = functools.reduce(lambda x, y: x + y, [jnp.exp(x - gmax) for x in gl])
    gw = 1.0 / gsum
    sel, taken = [], jnp.zeros_like(gmax, dtype=jnp.bool_)
    for g in range(N_GROUPS):
        s = jnp.logical_and(gl[g] == gmax, jnp.logical_not(taken))
        sel.append(s)
        taken = jnp.logical_or(taken, s)
    el = []
    for j in range(EPG):
        e = jnp.zeros_like(gmax)
        for g in range(N_GROUPS):
            e = jnp.where(sel[g], lt[N_GROUPS + g * EPG + j:N_GROUPS + g * EPG + j + 1, :], e)
        el.append(e)
    neg = jnp.full_like(gmax, -jnp.inf)
    top1 = functools.reduce(jnp.maximum, el)
    is1, taken = [], jnp.zeros_like(gmax, dtype=jnp.bool_)
    for j in range(EPG):
        s = jnp.logical_and(el[j] == top1, jnp.logical_not(taken))
        is1.append(s)
        taken = jnp.logical_or(taken, s)
    rest = [jnp.where(is1[j], neg, el[j]) for j in range(EPG)]
    top2 = functools.reduce(jnp.maximum, rest)
    is2, taken = [], jnp.zeros_like(gmax, dtype=jnp.bool_)
    for j in range(EPG):
        s = jnp.logical_and(jnp.logical_and(rest[j] == top2, jnp.logical_not(is1[j])),
                            jnp.logical_not(taken))
        is2.append(s)
        taken = jnp.logical_or(taken, s)
    w1 = 1.0 / (1.0 + jnp.exp(top2 - top1))
    w2 = 1.0 - w1
    comb = [jnp.where(is1[j], w1, jnp.where(is2[j], w2, 0.0)) * gw for j in range(EPG)]
    return sel, comb


def _moe_sort_kernel(h_ref, g_ref, wr2_ref, rb_ref, tri_ref, xs_ref, cs_ref, info_ref, seg_ref):
    vn = _rms_norm(h_ref[...], g_ref[...])
    v_hi, v_lo = _split_bf16(vn)
    wr2 = wr2_ref[...]
    r2 = _dot(v_hi, wr2)
    logits = r2[:, :LANES] + r2[:, LANES:] + _dot(v_lo, wr2[:, :LANES]) + rb_ref[...]
    sel, comb = _route(logits.T)

    onehot = jnp.concatenate([s.astype(F32) for s in sel] + [jnp.zeros((8 - N_GROUPS, TS), F32)], axis=0)
    cnt = _dot(onehot.astype(BF16), tri_ref[...])
    totals = [jnp.max(cnt[g:g + 1, :], axis=1, keepdims=True) for g in range(N_GROUPS)]
    starts = [jnp.zeros((1, 1), F32)]
    for g in range(N_GROUPS - 1):
        starts.append(starts[g] + totals[g])
    slot = jnp.zeros((1, TS), F32)
    for g in range(N_GROUPS):
        slot = jnp.where(sel[g], starts[g] + cnt[g:g + 1, :] - 1.0, slot)

    info = jnp.concatenate(comb + [slot, jnp.zeros((LANES - EPG - 1, TS), F32)], axis=0).T
    info_ref[...] = info

    srow = lax.broadcasted_iota(jnp.int32, (TS, TS), 0)
    perm = jnp.where(srow == slot.astype(jnp.int32), 1.0, 0.0).astype(BF16)
    lane = lax.broadcasted_iota(jnp.int32, (TS, LANES), 1)
    c_hi = info.astype(BF16).astype(F32)
    c_lo = (info - c_hi).astype(BF16).astype(F32)
    pack = jnp.where(lane < EPG, c_hi, pltpu.roll(c_lo, EPG, 1)).astype(BF16)
    xs_ext = _dot(perm, jnp.concatenate([v_hi, pack], axis=1))
    xs_ref[...] = xs_ext[:, :D].astype(BF16)
    cs = xs_ext[:, D:]
    cs_ref[...] = cs + pltpu.roll(cs, LANES - EPG, 1)

    lane1 = lax.broadcasted_iota(jnp.int32, (1, LANES), 1)
    seg = jnp.zeros((1, LANES), F32)
    for g in range(N_GROUPS):
        seg = jnp.where(lane1 == g, starts[g], seg)
        seg = jnp.where(lane1 == N_GROUPS + g, starts[g] + totals[g], seg)
    seg_ref[0] = seg.astype(jnp.int32)


def _moe_sort(h, g, wr2, rb, tri):
    tp = h.shape[0]
    nblk = tp // TS
    const = lambda x: pl.BlockSpec(x.shape, lambda i: (0, 0))
    row = lambda w: pl.BlockSpec((TS, w), lambda i: (i, 0))
    return pl.pallas_call(
        _moe_sort_kernel,
        grid=(nblk,),
        in_specs=[row(D), const(g), const(wr2), const(rb), const(tri)],
        out_specs=[row(D), row(LANES), row(LANES), pl.BlockSpec((1, 1, LANES), lambda i: (i, 0, 0))],
        out_shape=[jax.ShapeDtypeStruct((tp, D), BF16), jax.ShapeDtypeStruct((tp, LANES), F32),
                   jax.ShapeDtypeStruct((tp, LANES), F32), jax.ShapeDtypeStruct((nblk, 1, LANES), jnp.int32)],
        compiler_params=pltpu.CompilerParams(dimension_semantics=("arbitrary",),
                                             vmem_limit_bytes=VMEM_LIMIT),
        name="moe_sort",
    )(h, g, wr2, rb, tri)


def _moe_ffn_kernel(seg_ref, xs_ref, cs_ref, info_ref, h_ref, w1_ref, w3_ref, w2_ref, out_ref, ys_ref):
    i = pl.program_id(0)
    ys_ref[...] = jnp.zeros_like(ys_ref)

    def window(g, start, end, lo, size):
        ws = pl.multiple_of(jnp.minimum(lo, TS - size), SEG_ALIGN)
        rows = pl.ds(ws, size)
        x = xs_ref[rows, :]
        h1 = _dot(x, w1_ref[g])
        h3 = _dot(x, w3_ref[g])
        he = h1 * jax.nn.sigmoid(h1) * h3
        r = ws + lax.broadcasted_iota(jnp.int32, (size, 1), 0)
        inseg = jnp.logical_and(r >= jnp.maximum(start, lo), r < jnp.minimum(end, lo + size))
        cs = cs_ref[rows, :]
        parts = []
        for j in range(EPG):
            wcol = jnp.where(inseg, cs[:, j:j + 1], 0.0)
            parts.append((he[:, j * EXPERT_HIDDEN:(j + 1) * EXPERT_HIDDEN] * wcol).astype(BF16))
        ys_ref[rows, :] += _dot(jnp.concatenate(parts, axis=1), w2_ref[g])

    def group(g, carry):
        start = seg_ref[i * LANES + g]
        end = seg_ref[i * LANES + N_GROUPS + g]
        first = (start // SEG_ALIGN) * SEG_ALIGN
        need = jnp.where(end > start, end - first, 0)
        n_full = jnp.maximum(need - 1, 0) // SEG_SIZES[-1]

        def full(w, c2):
            window(g, start, end, first + w * SEG_SIZES[-1], SEG_SIZES[-1])
            return c2

        lax.fori_loop(0, n_full, full, 0)
        rest = need - n_full * SEG_SIZES[-1]
        for n, size in enumerate(SEG_SIZES):
            fits = jnp.logical_and(rest > (SEG_SIZES[n - 1] if n else 0), rest <= size)

            @pl.when(fits)
            def _():
                window(g, start, end, first + n_full * SEG_SIZES[-1], size)

        return carry

    lax.fori_loop(0, N_GROUPS, group, 0)

    slot = info_ref[:, SLOT_LANE:SLOT_LANE + 1].astype(jnp.int32)
    scol = lax.broadcasted_iota(jnp.int32, (TS, TS), 1)
    unperm = jnp.where(scol == slot, 1.0, 0.0).astype(BF16)
    out_ref[...] = h_ref[...] + _dot(unperm, ys_ref[...].astype(BF16))


def _moe_ffn(seg, xs, cs, info, h, w1g, w3g, w2g):
    tp = h.shape[0]
    nblk = tp // TS
    row = lambda w: pl.BlockSpec((TS, w), lambda i, s: (i, 0))
    resident = lambda x: pl.BlockSpec(x.shape, lambda i, s: (0, 0, 0), pipeline_mode=pl.Buffered(1))
    return pl.pallas_call(
        _moe_ffn_kernel,
        grid_spec=pltpu.PrefetchScalarGridSpec(
            num_scalar_prefetch=1,
            grid=(nblk,),
            in_specs=[row(D), row(LANES), row(LANES), row(D), resident(w1g), resident(w3g), resident(w2g)],
            out_specs=row(D),
            scratch_shapes=[pltpu.VMEM((TS, D), F32)]),
        out_shape=jax.ShapeDtypeStruct((tp, D), F32),
        compiler_params=pltpu.CompilerParams(dimension_semantics=("arbitrary",),
                                             vmem_limit_bytes=MOE_VMEM_LIMIT),
        name="moe_ffn",
    )(seg, xs, cs, info, h, w1g, w3g, w2g)


def _final_norm_kernel(h_ref, g_ref, out_ref):
    out_ref[...] = _rms_norm(h_ref[...], g_ref[...])


def _final_norm(h, g, batch, seq_out):
    nblk = seq_out // TM
    return pl.pallas_call(
        _final_norm_kernel,
        grid=(batch, nblk),
        in_specs=[pl.BlockSpec((pl.Element(TM), pl.Element(D)),
                               lambda b, r: (pl.multiple_of(b * LP + N_META + r * TM, N_META), 0)),
                  pl.BlockSpec((1, D), lambda b, r: (0, 0))],
        out_specs=pl.BlockSpec((TM, D), lambda b, r: (b * nblk + r, 0)),
        out_shape=jax.ShapeDtypeStruct((batch * seq_out, D), F32),
        compiler_params=pltpu.CompilerParams(dimension_semantics=("arbitrary", "arbitrary")),
        name="final_norm",
    )(h, g)


def _pad_lanes(x, n=LANES):
    return jnp.pad(x, ((0, 0), (0, n - x.shape[1])))


def _pad_rows(w, n):
    return jnp.pad(w, ((0, n - w.shape[0]), (0, 0)))


def _hi_lo_weight(w):
    hi = w.astype(BF16)
    lo = (w - hi.astype(F32)).astype(BF16)
    return jnp.concatenate([_pad_lanes(hi), _pad_lanes(lo)], axis=1)


def _per_head_cols(w):
    k = w.shape[0]
    w = w.reshape(k, HEADS, HEAD_DIM)
    return jnp.pad(w, ((0, 0), (0, 0), (0, HEAD_PAD - HEAD_DIM))).reshape(k, ATT_W)


def _bias_select():
    sel = jnp.zeros((LANES, 2, HEADS, HEAD_PAD), F32)
    one_lane = N_FPIECE * HEADS
    for p in range(N_FPIECE):
        for hd in range(HEADS):
            sel = sel.at[p * HEADS + hd, 0, hd, HEAD_DIM + p].set(1.0)
            sel = sel.at[p * HEADS + hd, 1, hd, HEAD_DIM + N_FPIECE + p].set(-1.0)
    sel = sel.at[one_lane, 0, :, HEAD_DIM + N_FPIECE:HEAD_DIM + 2 * N_FPIECE].set(1.0)
    sel = sel.at[one_lane, 1, :, HEAD_DIM:HEAD_DIM + N_FPIECE].set(1.0)
    return sel.reshape(LANES, 2 * ATT_W).astype(BF16)


def kernel(x, meta, norm1_g, w_in, b_forget, pool_w, pool_b, pool_scale, conv_w, conv_b, conv_ln_g,
           conv_ln_b, w_out_a, w_out_b, w_out_c, w_o, norm2_g, router_g, router_g_b, router_e,
           router_e_b, exp_w1, exp_w3, exp_w2, final_g):
    B, L, _ = x.shape
    depth = w_in.shape[0]
    assert LP == L + TM and LP % TQ == 0 and meta.shape[0] == N_META
    assert (B * LP) % TS == 0 and (B * LP) % TM_OUT == 0
    h = (x.reshape(B * L, D), meta.astype(x.dtype))
    tri = jnp.tril(jnp.ones((TM, TM), BF16))
    tri_u = jnp.triu(jnp.ones((TS, TS), BF16))
    sel = _bias_select()

    c_abc = POOL_W + 2 * CONV_W
    c_q = c_abc + FOX_W
    c_k = c_q + FOX_W
    c_v = c_k + FOX_W
    c_f = c_v + HEADS
    for l in range(depth):
        wl = w_in[l]
        wabc = wl[:, :c_abc].astype(BF16)
        wqk = jnp.concatenate([wl[:, c_abc:c_q] * (HEAD_DIM ** -0.5 * LOG2E), wl[:, c_q:c_k]],
                              axis=1).astype(BF16)
        wvt = _per_head_cols(wl[:, c_k:c_v]).T.astype(BF16)
        wf = _pad_lanes(wl[:, c_v:c_f]).astype(BF16)
        wg = wl[:, c_f:].astype(BF16)
        bf = _pad_lanes(b_forget[l][None, :])
        pw = jax.scipy.linalg.block_diag(*[pool_w[l, g] for g in range(pool_w.shape[1])]).astype(BF16)
        pool_conv = (pw, pool_b[l].reshape(1, POOL_W), pool_scale[l][None, :], _pad_rows(conv_w[l], 32),
                     conv_b[l][None, :], conv_ln_g[l][None, :], conv_ln_b[l][None, :])
        ya, yb, q, k, vt, *h0 = _in_proj(h, norm1_g[l][None, :], wabc, wqk, wvt, wf, bf, tri, sel, pool_conv, B)
        if h0:
            h = h0[0]
        o = _attention(q, k, vt, B)
        h = _mixer_out(h, norm1_g[l][None, :], ya, yb, o, wg, w_out_a[l].astype(BF16),
                       w_out_b[l].astype(BF16), w_out_c[l].astype(BF16), w_o[l].astype(BF16))

        wr2 = _hi_lo_weight(jnp.concatenate([router_g[l], router_e[l]], axis=1))
        rb = _pad_lanes(jnp.concatenate([router_g_b[l], router_e_b[l]])[None, :])
        grouped = lambda w: (w.reshape(N_GROUPS, EPG, D, EXPERT_HIDDEN).transpose(0, 2, 1, 3)
                             .reshape(N_GROUPS, D, GROUP_HIDDEN).astype(BF16))
        xs, cs, info, seg = _moe_sort(h, norm2_g[l][None, :], wr2, rb, tri_u)
        h = _moe_ffn(seg.reshape(-1), xs, cs, info, h, grouped(exp_w1[l]), grouped(exp_w3[l]),
                     exp_w2[l].reshape(N_GROUPS, GROUP_HIDDEN, D).astype(BF16))

    assert L % TM == 0
    return _final_norm(h, final_g[None, :], B, L).reshape(B, L, D)
```
